```python
import math
import jax, jax.numpy as jnp
from jax import lax
import numpy as np

D_MODEL = 2048
BATCH = 2
SEQ = 8192
DEPTH = 2

CHUNK = 64
N_MIXERS = 2
EPS = 1e-6

A_HEADS = 16
A_HEAD_DIM = 128
A_INNER = A_HEADS * A_HEAD_DIM
CONV_K = 4

B_HEADS = 16
B_HEAD_DIM = 128
B_INNER = B_HEADS * B_HEAD_DIM
LEFT_CHUNKS = 8
BAND = (LEFT_CHUNKS + 1) * CHUNK
REL_CLIP = 256

kernel_name = "hybrid_gdn_chunkband_stream"


def rms_norm(x, w):
    xf = x.astype(jnp.float32)
    y = xf * lax.rsqrt(jnp.mean(xf * xf, axis=-1, keepdims=True) + EPS)
    return (y * w.astype(jnp.float32)).astype(x.dtype)


def l2_norm(x):
    xf = x.astype(jnp.float32)
    return xf * lax.rsqrt(jnp.sum(xf * xf, axis=-1, keepdims=True) + EPS)


def causal_depthwise_conv(x, w):
    c = x.shape[-1]
    return lax.conv_general_dilated(
        x, w[:, None, :].astype(x.dtype), window_strides=(1,),
        padding=[(CONV_K - 1, 0)], dimension_numbers=("NWC", "WIO", "NWC"),
        feature_group_count=c)


def gated_deltanet(h, w_in, conv_w, a_log, dt_bias, out_norm_w, w_out):
    bsz, t, _ = h.shape
    nc = t // CHUNK
    proj = h @ w_in.astype(h.dtype)
    qkv = proj[..., :3 * A_INNER]
    z = proj[..., 3 * A_INNER:4 * A_INNER]
    a_in = proj[..., 4 * A_INNER:4 * A_INNER + A_HEADS]
    b_in = proj[..., 4 * A_INNER + A_HEADS:]
    qkv = jax.nn.silu(causal_depthwise_conv(qkv, conv_w))
    q = qkv[..., :A_INNER].reshape(bsz, t, A_HEADS, A_HEAD_DIM)
    k = qkv[..., A_INNER:2 * A_INNER].reshape(bsz, t, A_HEADS, A_HEAD_DIM)
    v = qkv[..., 2 * A_INNER:].reshape(bsz, t, A_HEADS, A_HEAD_DIM).astype(jnp.float32)
    q = l2_norm(q) * (A_HEAD_DIM ** -0.5)
    k = l2_norm(k)
    beta = jax.nn.sigmoid(b_in.astype(jnp.float32))
    g = -jnp.exp(a_log.astype(jnp.float32)) * jax.nn.softplus(
        a_in.astype(jnp.float32) + dt_bias.astype(jnp.float32))

    def to_chunks(u):
        u = u.reshape((bsz, nc, CHUNK) + u.shape[2:])
        return jnp.moveaxis(u, 3, 1)

    q, k, v = to_chunks(q), to_chunks(k), to_chunks(v)
    beta, g = to_chunks(beta), to_chunks(g)
    gc = jnp.cumsum(g, axis=-1)
    tri_incl = jnp.tril(jnp.ones((CHUNK, CHUNK), dtype=bool))
    tri_strict = jnp.tril(jnp.ones((CHUNK, CHUNK), dtype=bool), k=-1)
    diff = gc[..., :, None] - gc[..., None, :]
    decay = jnp.exp(jnp.where(tri_incl, diff, -jnp.inf))

    k_beta = k * beta[..., None]
    v_beta = v * beta[..., None]
    lower = jnp.where(tri_strict, jnp.einsum('bhncd,bhnsd->bhncs', k_beta, k) * decay, 0.0)
    eye = jnp.eye(CHUNK, dtype=jnp.float32)
    rhs = jnp.concatenate([v_beta, k_beta * jnp.exp(gc)[..., None]], axis=-1)
    sol = lax.linalg.triangular_solve(eye + lower, rhs, left_side=True, lower=True,
                                      unit_diagonal=True)
    u = sol[..., :A_HEAD_DIM]
    w = sol[..., A_HEAD_DIM:]
    qk = jnp.einsum('bhncd,bhnsd->bhncs', q, k) * decay
    q_dec = q * jnp.exp(gc)[..., None]
    k_dec = k * jnp.exp(gc[..., -1:] - gc)[..., None]
    g_last = jnp.exp(gc[..., -1])

    def step(state, inp):
        qk_n, q_dec_n, k_dec_n, u_n, w_n, gl_n = inp
        v_new = u_n - jnp.einsum('bhcd,bhdv->bhcv', w_n, state)
        o_n = (jnp.einsum('bhcd,bhdv->bhcv', q_dec_n, state)
               + jnp.einsum('bhcs,bhsv->bhcv', qk_n, v_new))
        state = state * gl_n[..., None, None] + jnp.einsum('bhcd,bhcv->bhdv', k_dec_n, v_new)
        return state, o_n

    xs = tuple(jnp.moveaxis(a, 2, 0) for a in (qk, q_dec, k_dec, u, w, g_last))
    s0 = jnp.zeros((bsz, A_HEADS, A_HEAD_DIM, A_HEAD_DIM), jnp.float32)
    _, o = lax.scan(step, s0, xs)
    o = jnp.transpose(o, (1, 0, 3, 2, 4)).reshape(bsz, t, A_HEADS, A_HEAD_DIM)
    zg = jax.nn.silu(z.astype(jnp.float32)).reshape(bsz, t, A_HEADS, A_HEAD_DIM)
    o = rms_norm(o, out_norm_w) * zg
    return o.reshape(bsz, t, A_INNER).astype(h.dtype) @ w_out.astype(h.dtype)


def chunk_band_attention(h, w_in, q_norm_w, k_norm_w, rel_bias, w_out):
    bsz, t, _ = h.shape
    nc = t // CHUNK
    pad = LEFT_CHUNKS * CHUNK
    proj = h @ w_in.astype(h.dtype)
    q = rms_norm(proj[..., :B_INNER].reshape(bsz, t, B_HEADS, B_HEAD_DIM), q_norm_w)
    k = rms_norm(proj[..., B_INNER:2 * B_INNER].reshape(bsz, t, B_HEADS, B_HEAD_DIM), k_norm_w)
    v = proj[..., 2 * B_INNER:3 * B_INNER].reshape(bsz, t, B_HEADS, B_HEAD_DIM)
    z = proj[..., 3 * B_INNER:]
    k_pad = jnp.pad(k, ((0, 0), (pad, 0), (0, 0), (0, 0)))
    v_pad = jnp.pad(v, ((0, 0), (pad, 0), (0, 0), (0, 0)))
    q_chunks = jnp.moveaxis(q.reshape(bsz, nc, CHUNK, B_HEADS, B_HEAD_DIM), 1, 0)

    r = jnp.arange(CHUNK)
    m = jnp.arange(BAND)
    rel = (pad + r[:, None]) - m[None, :]
    idx = jnp.clip(rel, -REL_CLIP, REL_CLIP) + REL_CLIP
    bias = rel_bias.astype(jnp.float32)[:, idx]
    scale = B_HEAD_DIM ** -0.5

    def one_chunk(args):
        n, q_n = args
        start = n * CHUNK
        k_n = lax.dynamic_slice_in_dim(k_pad, start, BAND, axis=1)
        v_n = lax.dynamic_slice_in_dim(v_pad, start, BAND, axis=1)
        s = jnp.einsum('bchd,bmhd->bhcm', q_n, k_n).astype(jnp.float32) * scale + bias
        valid = (start - pad + m) >= 0
        s = jnp.where(valid[None, None, None, :], s, -jnp.inf)
        p = jax.nn.softmax(s, axis=-1).astype(v_n.dtype)
        return jnp.einsum('bhcm,bmhd->bchd', p, v_n)

    o = lax.map(one_chunk, (jnp.arange(nc, dtype=jnp.int32), q_chunks))
    o = jnp.moveaxis(o, 0, 1).reshape(bsz, t, B_INNER)
    o = (o.astype(jnp.float32) * jax.nn.silu(z.astype(jnp.float32))).astype(h.dtype)
    return o @ w_out.astype(h.dtype)


def setup_inputs(seed: int = 0) -> dict:
    key = jax.random.key(seed)
    ks = jax.random.split(key, 16)
    n_a = (DEPTH + 1) // N_MIXERS
    n_b = DEPTH // N_MIXERS
    f32 = jnp.float32
    x = jax.random.normal(ks[0], (BATCH, SEQ, D_MODEL), f32)
    norm_w = 1.0 + 0.02 * jax.random.normal(ks[1], (DEPTH, D_MODEL), f32)
    a_w_in = jax.random.normal(ks[2], (n_a, D_MODEL, 4 * A_INNER + 2 * A_HEADS), f32) * D_MODEL ** -0.5
    a_conv_w = jax.random.normal(ks[3], (n_a, CONV_K, 3 * A_INNER), f32) * CONV_K ** -0.5
    a_a_log = jnp.log(jax.random.uniform(ks[4], (n_a, A_HEADS), f32, 1.0, 16.0))
    dt = jnp.exp(jax.random.uniform(ks[5], (n_a, A_HEADS), f32, math.log(1e-3), math.log(1e-1)))
    a_dt_bias = dt + jnp.log(-jnp.expm1(-dt))
    a_out_norm_w = 1.0 + 0.02 * jax.random.normal(ks[6], (n_a, A_HEAD_DIM), f32)
    a_w_out = jax.random.normal(ks[7], (n_a, A_INNER, D_MODEL), f32) * A_INNER ** -0.5
    b_w_in = jax.random.normal(ks[8], (n_b, D_MODEL, 4 * B_INNER), f32) * D_MODEL ** -0.5
    b_q_norm_w = 1.0 + 0.02 * jax.random.normal(ks[9], (n_b, B_HEAD_DIM), f32)
    b_k_norm_w = 1.0 + 0.02 * jax.random.normal(ks[10], (n_b, B_HEAD_DIM), f32)
    b_rel_bias = 0.5 * jax.random.normal(ks[11], (n_b, B_HEADS, 2 * REL_CLIP + 1), f32)
    b_w_out = jax.random.normal(ks[12], (n_b, B_INNER, D_MODEL), f32) * B_INNER ** -0.5
    return {"x": x, "norm_w": norm_w, "a_w_in": a_w_in, "a_conv_w": a_conv_w,
            "a_a_log": a_a_log, "a_dt_bias": a_dt_bias, "a_out_norm_w": a_out_norm_w,
            "a_w_out": a_w_out, "b_w_in": b_w_in, "b_q_norm_w": b_q_norm_w,
            "b_k_norm_w": b_k_norm_w, "b_rel_bias": b_rel_bias, "b_w_out": b_w_out}


def reference(x, norm_w, a_w_in, a_conv_w, a_a_log, a_dt_bias, a_out_norm_w, a_w_out,
              b_w_in, b_q_norm_w, b_k_norm_w, b_rel_bias, b_w_out):
    h = x
    for i in range(DEPTH):
        j = i // N_MIXERS
        hn = rms_norm(h, norm_w[i])
        if i % N_MIXERS == 0:
            y = gated_deltanet(hn, a_w_in[j], a_conv_w[j], a_a_log[j], a_dt_bias[j],
                               a_out_norm_w[j], a_w_out[j])
        else:
            y = chunk_band_attention(hn, b_w_in[j], b_q_norm_w[j], b_k_norm_w[j],
                                     b_rel_bias[j], b_w_out[j])
        h = h + y
    return h
```

```python
import functools

import jax
import jax.numpy as jnp
from jax import lax
from jax.experimental import pallas as pl
from jax.experimental.pallas import tpu as pltpu

F32 = jnp.float32
BF16 = jnp.bfloat16

EPS = 1e-6
CHUNK = 64
HEADS = 16
HEAD_DIM = 128
INNER = HEADS * HEAD_DIM
CONV_K = 4
LEFT_CHUNKS = 8
LEFT = LEFT_CHUNKS * CHUNK
BAND = LEFT + CHUNK
REL_CLIP = 256
NEG = -1e30

LANES = 128
SUBLANES = 8
VMEM_LIMIT = 48 * 1024 * 1024

GDN_ROWS = 512
ATT_ROWS = 512
BAND_PAD = BAND + CHUNK


def _dot(a, b):
    return jnp.dot(a.astype(BF16), b.astype(BF16), preferred_element_type=F32)


def _dot_nt(a, b):
    return lax.dot_general(a.astype(BF16), b.astype(BF16), (((1,), (1,)), ((), ())),
                           preferred_element_type=F32)


def _dot_tn(a, b):
    return lax.dot_general(a.astype(BF16), b.astype(BF16), (((0,), (0,)), ((), ())),
                           preferred_element_type=F32)


def _silu(x):
    return x * jax.nn.sigmoid(x)


def _softplus(x):
    return jnp.maximum(x, 0.0) + jnp.log1p(jnp.exp(-jnp.abs(x)))


def _rms_proj_body(x_ref, nw_ref, w_ref, *rest, has_tail):
    if has_tail:
        wt_ref, o_ref, ot_ref, hn_ref = rest
    else:
        o_ref, hn_ref = rest

    @pl.when(pl.program_id(1) == 0)
    def _():
        x = x_ref[...]
        ms = jnp.mean(x * x, axis=-1, keepdims=True)
        hn = (x * lax.rsqrt(ms + EPS) * nw_ref[...]).astype(BF16)
        hn_ref[...] = hn
        if has_tail:
            ot_ref[...] = jnp.dot(hn, wt_ref[...], preferred_element_type=F32)

    o_ref[...] = jnp.dot(hn_ref[...], w_ref[...], preferred_element_type=F32).astype(o_ref.dtype)


def _rms_proj(x, nw, w, w_tail=None, *, tm=512, tn=1024, out_dtype=F32):
    m, d = x.shape
    n = w.shape[1]
    assert m % tm == 0 and n % tn == 0
    has_tail = w_tail is not None
    in_specs = [pl.BlockSpec((tm, d), lambda i, j: (i, 0)),
                pl.BlockSpec((1, d), lambda i, j: (0, 0)),
                pl.BlockSpec((d, tn), lambda i, j: (0, j))]
    out_specs = [pl.BlockSpec((tm, tn), lambda i, j: (i, j))]
    out_shape = [jax.ShapeDtypeStruct((m, n), out_dtype)]
    args = [x, nw.reshape(1, d), w]
    if has_tail:
        nt = w_tail.shape[1]
        in_specs.append(pl.BlockSpec((d, nt), lambda i, j: (0, 0)))
        out_specs.append(pl.BlockSpec((tm, nt), lambda i, j: (i, 0)))
        out_shape.append(jax.ShapeDtypeStruct((m, nt), F32))
        args.append(w_tail)
    res = pl.pallas_call(
        functools.partial(_rms_proj_body, has_tail=has_tail),
        name="rms_proj_tail" if has_tail else "rms_proj",
        grid=(m // tm, n // tn),
        in_specs=in_specs, out_specs=out_specs, out_shape=out_shape,
        scratch_shapes=[pltpu.VMEM((tm, d), BF16)],
        compiler_params=pltpu.CompilerParams(
            dimension_semantics=("parallel", "arbitrary"), vmem_limit_bytes=VMEM_LIMIT),
    )(*args)
    return res if has_tail else res[0]


def _out_proj_body(a_ref, w_ref, r_ref, o_ref):
    o_ref[...] = r_ref[...] + jnp.dot(a_ref[...], w_ref[...], preferred_element_type=F32)


def _out_proj(a, w, res, *, tm=512):
    m, k = a.shape
    n = w.shape[1]
    assert m % tm == 0
    return pl.pallas_call(
        _out_proj_body,
        name="out_proj",
        grid=(m // tm,),
        in_specs=[pl.BlockSpec((tm, k), lambda i: (i, 0)),
                  pl.BlockSpec((k, n), lambda i: (0, 0)),
                  pl.BlockSpec((tm, n), lambda i: (i, 0))],
        out_specs=pl.BlockSpec((tm, n), lambda i: (i, 0)),
        out_shape=jax.ShapeDtypeStruct((m, n), F32),
        compiler_params=pltpu.CompilerParams(
            dimension_semantics=("parallel",), vmem_limit_bytes=VMEM_LIMIT),
    )(a, w, res)


def _unit_lower_inverse(low, row, col):
    eye = (row == col).astype(F32)
    base = SUBLANES
    ld = jnp.where((row // base) == (col // base), low, 0.0)
    l2 = _dot(ld, ld)
    l4 = _dot(l2, l2)
    x = eye - ld
    x = x + _dot(x, l2)
    x = x + _dot(x, l4)
    size = base
    while size < CHUNK:
        pair = 2 * size
        off = ((row // pair) == (col // pair)) & ((row % pair) >= size) & ((col % pair) < size)
        c = jnp.where(off, low, 0.0)
        x = x - _dot(x, _dot(c, x))
        size = pair
    return x


def _gdn_body(alog_ref, dtb_ref, q_ref, k_ref, v_ref, z_ref, cwq_ref, cwk_ref, cwv_ref,
              a_ref, b_ref, onw_ref, o_ref, xq_ref, xk_ref, xv_ref, s_ref):
    rows = q_ref.shape[0]
    head = pl.program_id(1)
    step = pl.program_id(2)

    @pl.when(step == 0)
    def _():
        zeros = jnp.zeros((SUBLANES, HEAD_DIM), F32)
        xq_ref[0:SUBLANES, :] = zeros
        xk_ref[0:SUBLANES, :] = zeros
        xv_ref[0:SUBLANES, :] = zeros
        s_ref[...] = jnp.zeros_like(s_ref)

    def conv_silu(x_ref, buf_ref, cw_ref):
        buf_ref[SUBLANES:SUBLANES + rows, :] = x_ref[...].astype(F32)
        w = cw_ref[...]
        acc = buf_ref[SUBLANES:SUBLANES + rows, :] * w[CONV_K - 1:CONV_K, :]
        for j in range(CONV_K - 1):
            shift = CONV_K - 1 - j
            acc = acc + buf_ref[SUBLANES - shift:SUBLANES - shift + rows, :] * w[j:j + 1, :]
        buf_ref[0:SUBLANES, :] = buf_ref[rows:rows + SUBLANES, :]
        return _silu(acc)

    q = conv_silu(q_ref, xq_ref, cwq_ref)
    k = conv_silu(k_ref, xk_ref, cwk_ref)
    v = conv_silu(v_ref, xv_ref, cwv_ref)
    q = q * lax.rsqrt(jnp.sum(q * q, axis=-1, keepdims=True) + EPS) * (HEAD_DIM ** -0.5)
    k = k * lax.rsqrt(jnp.sum(k * k, axis=-1, keepdims=True) + EPS)

    a_in = jnp.broadcast_to(a_ref[...], (SUBLANES, rows))
    b_in = jnp.broadcast_to(b_ref[...], (SUBLANES, rows))
    neg_rate = -jnp.exp(jnp.full((SUBLANES, rows), alog_ref[head], F32))
    g = neg_rate * _softplus(a_in + dtb_ref[head])
    beta = jax.nn.sigmoid(b_in)
    pos = lax.broadcasted_iota(jnp.int32, (SUBLANES, rows), 1) % CHUNK
    gc = g
    shift = 1
    while shift < CHUNK:
        gc = gc + jnp.where(pos >= shift, pltpu.roll(gc, shift, 1), 0.0)
        shift *= 2
    beta_col = jnp.transpose(beta[0:1, :])
    gc_col = jnp.transpose(gc[0:1, :])

    row = lax.broadcasted_iota(jnp.int32, (CHUNK, CHUNK), 0)
    col = lax.broadcasted_iota(jnp.int32, (CHUNK, CHUNK), 1)
    onw = onw_ref[...]

    state = s_ref[...]
    for c in range(rows // CHUNK):
        r0 = c * CHUNK
        qc, kc, vc = q[r0:r0 + CHUNK], k[r0:r0 + CHUNK], v[r0:r0 + CHUNK]
        bc = beta_col[r0:r0 + CHUNK]
        gcc = gc_col[r0:r0 + CHUNK]
        gcr = gc[0:1, r0:r0 + CHUNK]
        gcl = gc_col[r0 + CHUNK - 1:r0 + CHUNK]
        decay = jnp.exp(jnp.where(row >= col, gcc - gcr, -jnp.inf))
        egc = jnp.exp(gcc)
        kb = kc * bc
        low = jnp.where(row > col, _dot_nt(kb, kc) * decay, 0.0)
        inv = _unit_lower_inverse(low, row, col)
        sol = _dot(inv, jnp.concatenate([vc * bc, kb * egc], axis=1))
        u, w = sol[:, :HEAD_DIM], sol[:, HEAD_DIM:]
        qk = _dot_nt(qc, kc) * decay
        q_dec = qc * egc
        k_dec = kc * jnp.exp(gcl - gcc)

        both = _dot(jnp.concatenate([w, q_dec], axis=0), state)
        v_new = u - both[:CHUNK]
        o = both[CHUNK:] + _dot(qk, v_new)
        state = state * jnp.exp(gcl) + _dot_tn(k_dec, v_new)

        zg = _silu(z_ref[r0:r0 + CHUNK, :].astype(F32))
        o = o * lax.rsqrt(jnp.mean(o * o, axis=-1, keepdims=True) + EPS) * onw
        o_ref[r0:r0 + CHUNK, :] = (o * zg).astype(o_ref.dtype)
    s_ref[...] = state


def _gdn(proj, gates_t, conv_w, a_log, dt_bias, out_norm_w, batch, seq):
    rows = GDN_ROWS
    assert seq % rows == 0
    nt = seq // rows
    blk = lambda off: pl.BlockSpec((rows, HEAD_DIM), lambda b, h, t: (b * nt + t, off + h))
    cw = lambda off: pl.BlockSpec((CONV_K, HEAD_DIM), lambda b, h, t: (0, off + h))
    gate = lambda off: pl.BlockSpec((None, None, 1, rows), lambda b, h, t: (b, off + h, 0, t))
    smem = pl.BlockSpec(memory_space=pltpu.SMEM)
    return pl.pallas_call(
        _gdn_body,
        name="gdn",
        grid=(batch, HEADS, nt),
        in_specs=[smem, smem,
                  blk(0), blk(HEADS), blk(2 * HEADS), blk(3 * HEADS),
                  cw(0), cw(HEADS), cw(2 * HEADS),
                  gate(0), gate(HEADS),
                  pl.BlockSpec((1, HEAD_DIM), lambda b, h, t: (0, 0))],
        out_specs=pl.BlockSpec((rows, HEAD_DIM), lambda b, h, t: (b * nt + t, h)),
        out_shape=jax.ShapeDtypeStruct((batch * seq, INNER), BF16),
        scratch_shapes=[pltpu.VMEM((rows + SUBLANES, HEAD_DIM), F32)] * 3
                       + [pltpu.VMEM((HEAD_DIM, HEAD_DIM), F32)],
        compiler_params=pltpu.CompilerParams(
            dimension_semantics=("parallel", "parallel", "arbitrary"), vmem_limit_bytes=VMEM_LIMIT),
    )(a_log, dt_bias, proj, proj, proj, proj, conv_w, conv_w, conv_w, gates_t, gates_t,
      out_norm_w.reshape(1, HEAD_DIM))


def _attn_body(q_ref, k_ref, v_ref, z_ref, qnw_ref, knw_ref, bias_ref, o_ref, kbuf_ref, vbuf_ref, s_ref):
    rows = q_ref.shape[0]
    step = pl.program_id(2)

    @pl.when(step == 0)
    def _():
        kbuf_ref[0:LEFT, :] = jnp.zeros((LEFT, HEAD_DIM), BF16)
        vbuf_ref[0:LEFT, :] = jnp.zeros((LEFT, HEAD_DIM), BF16)

    def rms(x, w):
        return x * lax.rsqrt(jnp.mean(x * x, axis=-1, keepdims=True) + EPS) * w

    q = rms(q_ref[...].astype(F32), qnw_ref[...]) * (HEAD_DIM ** -0.5)
    kbuf_ref[LEFT:LEFT + rows, :] = rms(k_ref[...].astype(F32), knw_ref[...]).astype(BF16)
    vbuf_ref[LEFT:LEFT + rows, :] = v_ref[...].astype(BF16)
    s_ref[...] = _dot_nt(q, kbuf_ref[...])

    first_key = step * rows - LEFT
    col = lax.broadcasted_iota(jnp.int32, (CHUNK, BAND_PAD), 1)
    for c in range(rows // CHUNK):
        r0 = c * CHUNK
        lo = (c // 2) * LANES
        s = s_ref[r0:r0 + CHUNK, lo:lo + BAND_PAD] + bias_ref[c % 2]
        s = jnp.where(first_key + lo + col >= 0, s, NEG)
        p = jnp.exp(s - jnp.max(s, axis=-1, keepdims=True))
        denom = jnp.sum(p, axis=-1, keepdims=True)
        o = _dot(p, vbuf_ref[lo:lo + BAND_PAD, :]) / denom
        o = o * _silu(z_ref[r0:r0 + CHUNK, :].astype(F32))
        o_ref[r0:r0 + CHUNK, :] = o.astype(o_ref.dtype)

    kbuf_ref[0:LEFT, :] = kbuf_ref[rows:rows + LEFT, :]
    vbuf_ref[0:LEFT, :] = vbuf_ref[rows:rows + LEFT, :]


def _attn(proj, q_norm_w, k_norm_w, bias_tab, batch, seq):
    rows = ATT_ROWS
    assert seq % rows == 0 and rows >= LEFT and rows % LANES == 0
    nt = seq // rows
    blk = lambda off: pl.BlockSpec((rows, HEAD_DIM), lambda b, h, t: (b * nt + t, off + h))
    vec = pl.BlockSpec((1, HEAD_DIM), lambda b, h, t: (0, 0))
    return pl.pallas_call(
        _attn_body,
        name="band_attn",
        grid=(batch, HEADS, nt),
        in_specs=[blk(0), blk(HEADS), blk(2 * HEADS), blk(3 * HEADS), vec, vec,
                  pl.BlockSpec((None, 2, CHUNK, BAND_PAD), lambda b, h, t: (h, 0, 0, 0))],
        out_specs=pl.BlockSpec((rows, HEAD_DIM), lambda b, h, t: (b * nt + t, h)),
        out_shape=jax.ShapeDtypeStruct((batch * seq, INNER), BF16),
        scratch_shapes=[pltpu.VMEM((LEFT + rows, HEAD_DIM), BF16),
                        pltpu.VMEM((LEFT + rows, HEAD_DIM), BF16),
                        pltpu.VMEM((rows, LEFT + rows), F32)],
        compiler_params=pltpu.CompilerParams(
            dimension_semantics=("parallel", "parallel", "arbitrary"), vmem_limit_bytes=VMEM_LIMIT),
    )(proj, proj, proj, proj, q_norm_w.reshape(1, HEAD_DIM), k_norm_w.reshape(1, HEAD_DIM), bias_tab)


def _bias_table(rel_bias):
    r = jnp.arange(CHUNK)[:, None]
    m = jnp.arange(BAND)[None, :]
    idx = jnp.clip(LEFT + r - m, -REL_CLIP, REL_CLIP) + REL_CLIP
    band = rel_bias.astype(F32)[:, idx]
    pad = jnp.full((HEADS, CHUNK, CHUNK), NEG, F32)
    even = jnp.concatenate([band, pad], axis=-1)
    odd = jnp.concatenate([pad, band], axis=-1)
    return jnp.stack([even, odd], axis=1)


def kernel(x, norm_w, a_w_in, a_conv_w, a_a_log, a_dt_bias, a_out_norm_w, a_w_out,
           b_w_in, b_q_norm_w, b_k_norm_w, b_rel_bias, b_w_out):
    batch, seq, d = x.shape
    h0 = x.reshape(batch * seq, d)

    w_in = a_w_in[0]
    w_main = w_in[:, :4 * INNER].astype(BF16)
    w_tail = jnp.pad(w_in[:, 4 * INNER:], ((0, 0), (0, LANES - 2 * HEADS))).astype(BF16)
    proj, tail = _rms_proj(h0, norm_w[0], w_main, w_tail)
    gates_t = tail[:, :2 * HEADS].reshape(batch, seq, 2 * HEADS).transpose(0, 2, 1)
    gates_t = gates_t.reshape(batch, 2 * HEADS, 1, seq)
    mixed = _gdn(proj, gates_t, a_conv_w[0], a_a_log[0], a_dt_bias[0], a_out_norm_w[0], batch, seq)
    h1 = _out_proj(mixed, a_w_out[0].astype(BF16), h0)

    proj = _rms_proj(h1, norm_w[1], b_w_in[0].astype(BF16))
    mixed = _attn(proj, b_q_norm_w[0], b_k_norm_w[0], _bias_table(b_rel_bias[0]), batch, seq)
    h2 = _out_proj(mixed, b_w_out[0].astype(BF16), h1)
    return h2.reshape(batch, seq, d)
```

```python
import functools

import jax
import jax.numpy as jnp
from jax import lax
from jax.experimental import pallas as pl
from jax.experimental.pallas import tpu as pltpu

F32 = jnp.float32
BF16 = jnp.bfloat16

EPS = 1e-6
CHUNK = 64
HEADS = 16
HEAD_DIM = 128
INNER = HEADS * HEAD_DIM
CONV_K = 4
LEFT_CHUNKS = 8
LEFT = LEFT_CHUNKS * CHUNK
BAND = LEFT + CHUNK
REL_CLIP = 256
NEG = -1e30

LANES = 128
SUBLANES = 8
VMEM_LIMIT = 48 * 1024 * 1024

GDN_ROWS = 512
GDN_HEADS = 2
ATT_ROWS = 512
BAND_PAD = BAND + CHUNK


def _dot(a, b):
    return jnp.dot(a.astype(BF16), b.astype(BF16), preferred_element_type=F32)


def _dot_nt(a, b):
    return lax.dot_general(a.astype(BF16), b.astype(BF16), (((1,), (1,)), ((), ())),
                           preferred_element_type=F32)


def _dot_tn(a, b):
    return lax.dot_general(a.astype(BF16), b.astype(BF16), (((0,), (0,)), ((), ())),
                           preferred_element_type=F32)


def _silu(x):
    return x * jax.nn.sigmoid(x)


def _softplus(x):
    return jnp.maximum(x, 0.0) + jnp.log1p(jnp.exp(-jnp.abs(x)))


def _rms_proj_body(x_ref, nw_ref, w_ref, *rest, has_tail):
    if has_tail:
        wt_ref, o_ref, ot_ref, hn_ref = rest
    else:
        o_ref, hn_ref = rest

    @pl.when(pl.program_id(1) == 0)
    def _():
        x = x_ref[...]
        ms = jnp.mean(x * x, axis=-1, keepdims=True)
        hn = (x * lax.rsqrt(ms + EPS) * nw_ref[...]).astype(BF16)
        hn_ref[...] = hn
        if has_tail:
            ot_ref[...] = jnp.dot(hn, wt_ref[...], preferred_element_type=F32)

    o_ref[...] = jnp.dot(hn_ref[...], w_ref[...], preferred_element_type=F32).astype(o_ref.dtype)


def _rms_proj(x, nw, w, w_tail=None, *, tm=512, tn=1024, out_dtype=F32):
    m, d = x.shape
    n = w.shape[1]
    assert m % tm == 0 and n % tn == 0
    has_tail = w_tail is not None
    in_specs = [pl.BlockSpec((tm, d), lambda i, j: (i, 0)),
                pl.BlockSpec((1, d), lambda i, j: (0, 0)),
                pl.BlockSpec((d, tn), lambda i, j: (0, j))]
    out_specs = [pl.BlockSpec((tm, tn), lambda i, j: (i, j))]
    out_shape = [jax.ShapeDtypeStruct((m, n), out_dtype)]
    args = [x, nw.reshape(1, d), w]
    if has_tail:
        nt = w_tail.shape[1]
        in_specs.append(pl.BlockSpec((d, nt), lambda i, j: (0, 0)))
        out_specs.append(pl.BlockSpec((tm, nt), lambda i, j: (i, 0)))
        out_shape.append(jax.ShapeDtypeStruct((m, nt), F32))
        args.append(w_tail)
    res = pl.pallas_call(
        functools.partial(_rms_proj_body, has_tail=has_tail),
        name="rms_proj_tail" if has_tail else "rms_proj",
        grid=(m // tm, n // tn),
        in_specs=in_specs, out_specs=out_specs, out_shape=out_shape,
        scratch_shapes=[pltpu.VMEM((tm, d), BF16)],
        compiler_params=pltpu.CompilerParams(
            dimension_semantics=("parallel", "arbitrary"), vmem_limit_bytes=VMEM_LIMIT),
    )(*args)
    return res if has_tail else res[0]


def _out_proj_body(a_ref, w_ref, r_ref, o_ref):
    o_ref[...] = r_ref[...] + jnp.dot(a_ref[...], w_ref[...], preferred_element_type=F32)


def _out_proj(a, w, res, *, tm=512):
    m, k = a.shape
    n = w.shape[1]
    assert m % tm == 0
    return pl.pallas_call(
        _out_proj_body,
        name="out_proj",
        grid=(m // tm,),
        in_specs=[pl.BlockSpec((tm, k), lambda i: (i, 0)),
                  pl.BlockSpec((k, n), lambda i: (0, 0)),
                  pl.BlockSpec((tm, n), lambda i: (i, 0))],
        out_specs=pl.BlockSpec((tm, n), lambda i: (i, 0)),
        out_shape=jax.ShapeDtypeStruct((m, n), F32),
        compiler_params=pltpu.CompilerParams(
            dimension_semantics=("parallel",), vmem_limit_bytes=VMEM_LIMIT),
    )(a, w, res)


def _bdot(a, b):
    return jnp.einsum("cij,cjk->cik", a.astype(BF16), b.astype(BF16), preferred_element_type=F32)


def _bdot_nt(a, b):
    return jnp.einsum("cid,cjd->cij", a.astype(BF16), b.astype(BF16), preferred_element_type=F32)


def _unit_lower_inverse(low, row, col):
    eye = (row == col).astype(F32)
    base = SUBLANES
    ld = jnp.where((row // base) == (col // base), low, 0.0)
    l2 = _bdot(ld, ld)
    l4 = _bdot(l2, l2)
    x = eye - ld
    x = x + _bdot(x, l2)
    x = x + _bdot(x, l4)
    size = base
    while size < CHUNK:
        pair = 2 * size
        off = ((row // pair) == (col // pair)) & ((row % pair) >= size) & ((col % pair) < size)
        c = jnp.where(off, low, 0.0)
        x = x - _bdot(x, _bdot(c, x))
        size = pair
    return x


def _gdn_body(alog_ref, dtb_ref, q_ref, k_ref, v_ref, z_ref, cwq_ref, cwk_ref, cwv_ref,
              a_ref, b_ref, onw_ref, o_ref, xq_ref, xk_ref, xv_ref, s_ref):
    rows, width = q_ref.shape
    heads = width // HEAD_DIM
    chunks = rows // CHUNK
    head0 = pl.program_id(1) * heads
    step = pl.program_id(2)

    @pl.when(step == 0)
    def _():
        zeros = jnp.zeros((SUBLANES, width), F32)
        xq_ref[0:SUBLANES, :] = zeros
        xk_ref[0:SUBLANES, :] = zeros
        xv_ref[0:SUBLANES, :] = zeros
        s_ref[...] = jnp.zeros_like(s_ref)

    def conv_silu(x_ref, buf_ref, cw_ref):
        buf_ref[SUBLANES:SUBLANES + rows, :] = x_ref[...].astype(F32)
        w = cw_ref[...]
        acc = buf_ref[SUBLANES:SUBLANES + rows, :] * w[CONV_K - 1:CONV_K, :]
        for j in range(CONV_K - 1):
            shift = CONV_K - 1 - j
            acc = acc + buf_ref[SUBLANES - shift:SUBLANES - shift + rows, :] * w[j:j + 1, :]
        buf_ref[0:SUBLANES, :] = buf_ref[rows:rows + SUBLANES, :]
        return _silu(acc)

    q_all = conv_silu(q_ref, xq_ref, cwq_ref)
    k_all = conv_silu(k_ref, xk_ref, cwk_ref)
    v_all = conv_silu(v_ref, xv_ref, cwv_ref)

    pos = lax.broadcasted_iota(jnp.int32, (SUBLANES, rows), 1) % CHUNK
    parts = {name: [] for name in ("q", "k", "v", "beta", "gc", "gcr")}
    for hd in range(heads):
        lanes = slice(hd * HEAD_DIM, (hd + 1) * HEAD_DIM)
        q, k, v = q_all[:, lanes], k_all[:, lanes], v_all[:, lanes]
        q = q * lax.rsqrt(jnp.sum(q * q, axis=-1, keepdims=True) + EPS) * (HEAD_DIM ** -0.5)
        k = k * lax.rsqrt(jnp.sum(k * k, axis=-1, keepdims=True) + EPS)
        a_in = jnp.broadcast_to(a_ref[hd], (SUBLANES, rows))
        b_in = jnp.broadcast_to(b_ref[hd], (SUBLANES, rows))
        neg_rate = -jnp.exp(jnp.full((SUBLANES, rows), alog_ref[head0 + hd], F32))
        g = neg_rate * _softplus(a_in + dtb_ref[head0 + hd])
        beta = jax.nn.sigmoid(b_in)
        gc = g
        shift = 1
        while shift < CHUNK:
            gc = gc + jnp.where(pos >= shift, pltpu.roll(gc, shift, 1), 0.0)
            shift *= 2
        parts["q"].append(q.reshape(chunks, CHUNK, HEAD_DIM))
        parts["k"].append(k.reshape(chunks, CHUNK, HEAD_DIM))
        parts["v"].append(v.reshape(chunks, CHUNK, HEAD_DIM))
        parts["beta"].append(jnp.transpose(beta[0:1, :]).reshape(chunks, CHUNK, 1))
        parts["gc"].append(jnp.transpose(gc[0:1, :]).reshape(chunks, CHUNK, 1))
        parts["gcr"].append(jnp.stack([gc[0:1, c * CHUNK:(c + 1) * CHUNK] for c in range(chunks)]))
    q, k, v, beta, gc, gcr = (jnp.concatenate(parts[n], axis=0) for n in ("q", "k", "v", "beta", "gc", "gcr"))

    row = lax.broadcasted_iota(jnp.int32, (CHUNK, CHUNK), 0)
    col = lax.broadcasted_iota(jnp.int32, (CHUNK, CHUNK), 1)
    gcl = gc[:, CHUNK - 1:CHUNK, :]
    decay = jnp.exp(jnp.where(row >= col, gc - gcr, -jnp.inf))
    egc = jnp.exp(gc)
    kb = k * beta
    low = jnp.where(row > col, _bdot_nt(kb, k) * decay, 0.0)
    inv = _unit_lower_inverse(low, row, col)
    sol = _bdot(inv, jnp.concatenate([v * beta, kb * egc], axis=-1))
    u, w = sol[:, :, :HEAD_DIM], sol[:, :, HEAD_DIM:]
    qk = _bdot_nt(q, k) * decay
    wq = jnp.concatenate([w, q * egc], axis=1)
    k_dec = k * jnp.exp(gcl - gc)
    carry = jnp.exp(gcl)

    onw = onw_ref[...]
    states = [s_ref[hd] for hd in range(heads)]
    for c in range(chunks):
        r0 = c * CHUNK
        for hd in range(heads):
            i = hd * chunks + c
            lanes = slice(hd * HEAD_DIM, (hd + 1) * HEAD_DIM)
            both = _dot(wq[i], states[hd])
            v_new = u[i] - both[:CHUNK]
            o = both[CHUNK:] + _dot(qk[i], v_new)
            states[hd] = states[hd] * carry[i] + _dot_tn(k_dec[i], v_new)
            zg = _silu(z_ref[r0:r0 + CHUNK, lanes].astype(F32))
            o = o * lax.rsqrt(jnp.mean(o * o, axis=-1, keepdims=True) + EPS) * onw
            o_ref[r0:r0 + CHUNK, lanes] = (o * zg).astype(o_ref.dtype)
    for hd in range(heads):
        s_ref[hd] = states[hd]


def _gdn(proj, gates_t, conv_w, a_log, dt_bias, out_norm_w, batch, seq):
    rows, heads = GDN_ROWS, GDN_HEADS
    assert seq % rows == 0 and HEADS % heads == 0
    nt, groups, width = seq // rows, HEADS // heads, heads * HEAD_DIM
    blk = lambda off: pl.BlockSpec((rows, width), lambda b, h, t: (b * nt + t, off * groups + h))
    cw = lambda off: pl.BlockSpec((CONV_K, width), lambda b, h, t: (0, off * groups + h))
    gate = lambda off: pl.BlockSpec((None, heads, 1, rows), lambda b, h, t: (b, off * groups + h, 0, t))
    smem = pl.BlockSpec(memory_space=pltpu.SMEM)
    return pl.pallas_call(
        _gdn_body,
        name="gdn",
        grid=(batch, groups, nt),
        in_specs=[smem, smem,
                  blk(0), blk(1), blk(2), blk(3),
                  cw(0), cw(1), cw(2),
                  gate(0), gate(1),
                  pl.BlockSpec((1, HEAD_DIM), lambda b, h, t: (0, 0))],
        out_specs=pl.BlockSpec((rows, width), lambda b, h, t: (b * nt + t, h)),
        out_shape=jax.ShapeDtypeStruct((batch * seq, INNER), BF16),
        scratch_shapes=[pltpu.VMEM((rows + SUBLANES, width), F32)] * 3
                       + [pltpu.VMEM((heads, HEAD_DIM, HEAD_DIM), F32)],
        compiler_params=pltpu.CompilerParams(
            dimension_semantics=("parallel", "parallel", "arbitrary"), vmem_limit_bytes=VMEM_LIMIT),
    )(a_log, dt_bias, proj, proj, proj, proj, conv_w, conv_w, conv_w, gates_t, gates_t,
      out_norm_w.reshape(1, HEAD_DIM))


def _attn_body(q_ref, k_ref, v_ref, z_ref, qnw_ref, knw_ref, bias_ref, o_ref, kbuf_ref, vbuf_ref, s_ref):
    rows = q_ref.shape[0]
    step = pl.program_id(2)

    @pl.when(step == 0)
    def _():
        kbuf_ref[0:LEFT, :] = jnp.zeros((LEFT, HEAD_DIM), BF16)
        vbuf_ref[0:LEFT, :] = jnp.zeros((LEFT, HEAD_DIM), BF16)

    def rms(x, w):
        return x * lax.rsqrt(jnp.mean(x * x, axis=-1, keepdims=True) + EPS) * w

    q = rms(q_ref[...].astype(F32), qnw_ref[...]) * (HEAD_DIM ** -0.5)
    kbuf_ref[LEFT:LEFT + rows, :] = rms(k_ref[...].astype(F32), knw_ref[...]).astype(BF16)
    vbuf_ref[LEFT:LEFT + rows, :] = v_ref[...].astype(BF16)
    s_ref[...] = _dot_nt(q, kbuf_ref[...])

    first_key = step * rows - LEFT
    col = lax.broadcasted_iota(jnp.int32, (CHUNK, BAND_PAD), 1)
    for c in range(rows // CHUNK):
        r0 = c * CHUNK
        lo = (c // 2) * LANES
        s = s_ref[r0:r0 + CHUNK, lo:lo + BAND_PAD] + bias_ref[c % 2]
        s = jnp.where(first_key + lo + col >= 0, s, NEG)
        p = jnp.exp(s - jnp.max(s, axis=-1, keepdims=True))
        denom = jnp.sum(p, axis=-1, keepdims=True)
        o = _dot(p, vbuf_ref[lo:lo + BAND_PAD, :]) / denom
        o = o * _silu(z_ref[r0:r0 + CHUNK, :].astype(F32))
        o_ref[r0:r0 + CHUNK, :] = o.astype(o_ref.dtype)

    kbuf_ref[0:LEFT, :] = kbuf_ref[rows:rows + LEFT, :]
    vbuf_ref[0:LEFT, :] = vbuf_ref[rows:rows + LEFT, :]


def _attn(proj, q_norm_w, k_norm_w, bias_tab, batch, seq):
    rows = ATT_ROWS
    assert seq % rows == 0 and rows >= LEFT and rows % LANES == 0
    nt = seq // rows
    blk = lambda off: pl.BlockSpec((rows, HEAD_DIM), lambda b, h, t: (b * nt + t, off + h))
    vec = pl.BlockSpec((1, HEAD_DIM), lambda b, h, t: (0, 0))
    return pl.pallas_call(
        _attn_body,
        name="band_attn",
        grid=(batch, HEADS, nt),
        in_specs=[blk(0), blk(HEADS), blk(2 * HEADS), blk(3 * HEADS), vec, vec,
                  pl.BlockSpec((None, 2, CHUNK, BAND_PAD), lambda b, h, t: (h, 0, 0, 0))],
        out_specs=pl.BlockSpec((rows, HEAD_DIM), lambda b, h, t: (b * nt + t, h)),
        out_shape=jax.ShapeDtypeStruct((batch * seq, INNER), BF16),
        scratch_shapes=[pltpu.VMEM((LEFT + rows, HEAD_DIM), BF16),
                        pltpu.VMEM((LEFT + rows, HEAD_DIM), BF16),
                        pltpu.VMEM((rows, LEFT + rows), F32)],
        compiler_params=pltpu.CompilerParams(
            dimension_semantics=("parallel", "parallel", "arbitrary"), vmem_limit_bytes=VMEM_LIMIT),
    )(proj, proj, proj, proj, q_norm_w.reshape(1, HEAD_DIM), k_norm_w.reshape(1, HEAD_DIM), bias_tab)


def _bias_table(rel_bias):
    r = jnp.arange(CHUNK)[:, None]
    m = jnp.arange(BAND)[None, :]
    idx = jnp.clip(LEFT + r - m, -REL_CLIP, REL_CLIP) + REL_CLIP
    band = rel_bias.astype(F32)[:, idx]
    pad = jnp.full((HEADS, CHUNK, CHUNK), NEG, F32)
    even = jnp.concatenate([band, pad], axis=-1)
    odd = jnp.concatenate([pad, band], axis=-1)
    return jnp.stack([even, odd], axis=1)


def kernel(x, norm_w, a_w_in, a_conv_w, a_a_log, a_dt_bias, a_out_norm_w, a_w_out,
           b_w_in, b_q_norm_w, b_k_norm_w, b_rel_bias, b_w_out):
    batch, seq, d = x.shape
    h0 = x.reshape(batch * seq, d)

    w_in = a_w_in[0]
    w_main = w_in[:, :4 * INNER].astype(BF16)
    w_tail = jnp.pad(w_in[:, 4 * INNER:], ((0, 0), (0, LANES - 2 * HEADS))).astype(BF16)
    proj, tail = _rms_proj(h0, norm_w[0], w_main, w_tail)
    gates_t = tail[:, :2 * HEADS].reshape(batch, seq, 2 * HEADS).transpose(0, 2, 1)
    gates_t = gates_t.reshape(batch, 2 * HEADS, 1, seq)
    mixed = _gdn(proj, gates_t, a_conv_w[0], a_a_log[0], a_dt_bias[0], a_out_norm_w[0], batch, seq)
    h1 = _out_proj(mixed, a_w_out[0].astype(BF16), h0)

    proj = _rms_proj(h1, norm_w[1], b_w_in[0].astype(BF16))
    mixed = _attn(proj, b_q_norm_w[0], b_k_norm_w[0], _bias_table(b_rel_bias[0]), batch, seq)
    h2 = _out_proj(mixed, b_w_out[0].astype(BF16), h1)
    return h2.reshape(batch, seq, d)
```

```python
import functools

import jax
import jax.numpy as jnp
from jax import lax
from jax.experimental import pallas as pl
from jax.experimental.pallas import tpu as pltpu

F32 = jnp.float32
BF16 = jnp.bfloat16

EPS = 1e-6
CHUNK = 64
HEADS = 16
HEAD_DIM = 128
INNER = HEADS * HEAD_DIM
CONV_K = 4
LEFT_CHUNKS = 8
LEFT = LEFT_CHUNKS * CHUNK
BAND = LEFT + CHUNK
REL_CLIP = 256
NEG = -1e30

LANES = 128
SUBLANES = 8
VMEM_LIMIT = 48 * 1024 * 1024

GDN_ROWS = 512
GDN_HEADS = 4
ATT_ROWS = 512
ATT_HEADS = 2
BAND_PAD = BAND + CHUNK


def _dot(a, b):
    return jnp.dot(a.astype(BF16), b.astype(BF16), preferred_element_type=F32)


def _dot_nt(a, b):
    return lax.dot_general(a.astype(BF16), b.astype(BF16), (((1,), (1,)), ((), ())),
                           preferred_element_type=F32)


def _silu(x):
    return x * jax.nn.sigmoid(x)


def _softplus(x):
    return jnp.maximum(x, 0.0) + jnp.log1p(jnp.exp(-jnp.abs(x)))


def _rms_proj_body(x_ref, nw_ref, w_ref, *rest, has_tail):
    if has_tail:
        wt_ref, o_ref, ot_ref, hn_ref = rest
    else:
        o_ref, hn_ref = rest

    @pl.when(pl.program_id(1) == 0)
    def _():
        x = x_ref[...]
        ms = jnp.mean(x * x, axis=-1, keepdims=True)
        hn = (x * lax.rsqrt(ms + EPS) * nw_ref[...]).astype(BF16)
        hn_ref[...] = hn
        if has_tail:
            ot_ref[...] = jnp.dot(hn, wt_ref[...], preferred_element_type=F32)

    o_ref[...] = jnp.dot(hn_ref[...], w_ref[...], preferred_element_type=F32).astype(o_ref.dtype)


def _rms_proj(x, nw, w, n, w_tail=None, *, tm=1024, tn=1024, out_dtype=BF16):
    m, d = x.shape
    assert m % tm == 0 and n % tn == 0 and n <= w.shape[1]
    has_tail = w_tail is not None
    in_specs = [pl.BlockSpec((tm, d), lambda i, j: (i, 0)),
                pl.BlockSpec((1, d), lambda i, j: (0, 0)),
                pl.BlockSpec((d, tn), lambda i, j: (0, j))]
    out_specs = [pl.BlockSpec((tm, tn), lambda i, j: (i, j))]
    out_shape = [jax.ShapeDtypeStruct((m, n), out_dtype)]
    args = [x, nw.reshape(1, d), w]
    if has_tail:
        nt = w_tail.shape[1]
        in_specs.append(pl.BlockSpec((d, nt), lambda i, j: (0, 0)))
        out_specs.append(pl.BlockSpec((tm, nt), lambda i, j: (i, 0)))
        out_shape.append(jax.ShapeDtypeStruct((m, nt), F32))
        args.append(w_tail)
    res = pl.pallas_call(
        functools.partial(_rms_proj_body, has_tail=has_tail),
        name="rms_proj_tail" if has_tail else "rms_proj",
        grid=(m // tm, n // tn),
        in_specs=in_specs, out_specs=out_specs, out_shape=out_shape,
        scratch_shapes=[pltpu.VMEM((tm, d), BF16)],
        compiler_params=pltpu.CompilerParams(
            dimension_semantics=("parallel", "arbitrary"), vmem_limit_bytes=VMEM_LIMIT),
    )(*args)
    return res if has_tail else res[0]


def _out_proj_body(a_ref, w_ref, r_ref, o_ref):
    o_ref[...] = r_ref[...] + jnp.dot(a_ref[...], w_ref[...], preferred_element_type=F32)


def _out_proj(a, w, res, *, tm=512):
    m, k = a.shape
    n = w.shape[1]
    assert m % tm == 0
    return pl.pallas_call(
        _out_proj_body,
        name="out_proj",
        grid=(m // tm,),
        in_specs=[pl.BlockSpec((tm, k), lambda i: (i, 0)),
                  pl.BlockSpec((k, n), lambda i: (0, 0)),
                  pl.BlockSpec((tm, n), lambda i: (i, 0))],
        out_specs=pl.BlockSpec((tm, n), lambda i: (i, 0)),
        out_shape=jax.ShapeDtypeStruct((m, n), F32),
        compiler_params=pltpu.CompilerParams(
            dimension_semantics=("parallel",), vmem_limit_bytes=VMEM_LIMIT),
    )(a, w, res)


def _bdot(a, b):
    return jnp.einsum("cij,cjk->cik", a.astype(BF16), b.astype(BF16), preferred_element_type=F32)


def _bdot_nt(a, b):
    return jnp.einsum("cid,cjd->cij", a.astype(BF16), b.astype(BF16), preferred_element_type=F32)


def _bdot_tn(a, b):
    return jnp.einsum("cjd,cjn->cdn", a.astype(BF16), b.astype(BF16), preferred_element_type=F32)


def _unit_lower_inverse(low, row, col):
    eye = (row == col).astype(F32)
    base = SUBLANES
    ld = jnp.where((row // base) == (col // base), low, 0.0)
    l2 = _bdot(ld, ld)
    l4 = _bdot(l2, l2)
    x = eye - ld
    x = x + _bdot(x, l2)
    x = x + _bdot(x, l4)
    size = base
    while size < CHUNK:
        pair = 2 * size
        off = ((row // pair) == (col // pair)) & ((row % pair) >= size) & ((col % pair) < size)
        c = jnp.where(off, low, 0.0)
        x = x - _bdot(x, _bdot(c, x))
        size = pair
    return x


def _gdn_body(alog_ref, dtb_ref, q_ref, k_ref, v_ref, z_ref, cwq_ref, cwk_ref, cwv_ref,
              a_ref, b_ref, onw_ref, o_ref, xq_ref, xk_ref, xv_ref, s_ref):
    rows, width = q_ref.shape
    heads = width // HEAD_DIM
    chunks = rows // CHUNK
    head0 = pl.program_id(1) * heads
    step = pl.program_id(2)

    @pl.when(step == 0)
    def _():
        zeros = jnp.zeros((SUBLANES, width), F32)
        xq_ref[0:SUBLANES, :] = zeros
        xk_ref[0:SUBLANES, :] = zeros
        xv_ref[0:SUBLANES, :] = zeros
        s_ref[...] = jnp.zeros_like(s_ref)

    def conv_silu(x_ref, buf_ref, cw_ref):
        buf_ref[SUBLANES:SUBLANES + rows, :] = x_ref[...].astype(F32)
        w = cw_ref[...]
        acc = buf_ref[SUBLANES:SUBLANES + rows, :] * w[CONV_K - 1:CONV_K, :]
        for j in range(CONV_K - 1):
            shift = CONV_K - 1 - j
            acc = acc + buf_ref[SUBLANES - shift:SUBLANES - shift + rows, :] * w[j:j + 1, :]
        buf_ref[0:SUBLANES, :] = buf_ref[rows:rows + SUBLANES, :]
        return _silu(acc)

    q_all = conv_silu(q_ref, xq_ref, cwq_ref)
    k_all = conv_silu(k_ref, xk_ref, cwk_ref)
    v_all = conv_silu(v_ref, xv_ref, cwv_ref)

    pos = lax.broadcasted_iota(jnp.int32, (SUBLANES, rows), 1) % CHUNK
    parts = {name: [] for name in ("q", "k", "v", "beta", "gc", "gcr")}
    for hd in range(heads):
        lanes = slice(hd * HEAD_DIM, (hd + 1) * HEAD_DIM)
        q, k, v = q_all[:, lanes], k_all[:, lanes], v_all[:, lanes]
        q = q * lax.rsqrt(jnp.sum(q * q, axis=-1, keepdims=True) + EPS) * (HEAD_DIM ** -0.5)
        k = k * lax.rsqrt(jnp.sum(k * k, axis=-1, keepdims=True) + EPS)
        a_in = jnp.broadcast_to(a_ref[hd], (SUBLANES, rows))
        b_in = jnp.broadcast_to(b_ref[hd], (SUBLANES, rows))
        neg_rate = -jnp.exp(jnp.full((SUBLANES, rows), alog_ref[head0 + hd], F32))
        g = neg_rate * _softplus(a_in + dtb_ref[head0 + hd])
        beta = jax.nn.sigmoid(b_in)
        gc = g
        shift = 1
        while shift < CHUNK:
            gc = gc + jnp.where(pos >= shift, pltpu.roll(gc, shift, 1), 0.0)
            shift *= 2
        parts["q"].append(q.reshape(chunks, CHUNK, HEAD_DIM))
        parts["k"].append(k.reshape(chunks, CHUNK, HEAD_DIM))
        parts["v"].append(v.reshape(chunks, CHUNK, HEAD_DIM))
        parts["beta"].append(jnp.transpose(beta[0:1, :]).reshape(chunks, CHUNK, 1))
        parts["gc"].append(jnp.transpose(gc[0:1, :]).reshape(chunks, CHUNK, 1))
        parts["gcr"].append(jnp.stack([gc[0:1, c * CHUNK:(c + 1) * CHUNK] for c in range(chunks)]))
    q, k, v, beta, gc, gcr = (jnp.concatenate(parts[n], axis=0) for n in ("q", "k", "v", "beta", "gc", "gcr"))

    row = lax.broadcasted_iota(jnp.int32, (CHUNK, CHUNK), 0)
    col = lax.broadcasted_iota(jnp.int32, (CHUNK, CHUNK), 1)
    gcl = gc[:, CHUNK - 1:CHUNK, :]
    decay = jnp.exp(jnp.where(row >= col, gc - gcr, -jnp.inf))
    egc = jnp.exp(gc)
    kb = k * beta
    low = jnp.where(row > col, _bdot_nt(kb, k) * decay, 0.0)
    inv = _unit_lower_inverse(low, row, col)
    sol = _bdot(inv, jnp.concatenate([v * beta, kb * egc], axis=-1))
    qk = _bdot_nt(q, k) * decay
    k_dec = k * jnp.exp(gcl - gc)
    carry = jnp.exp(gcl)
    kt = _bdot_tn(k_dec, sol)
    qs = _bdot(qk, sol)
    lhs = jnp.concatenate([kt[:, :, HEAD_DIM:], q * egc - qs[:, :, HEAD_DIM:]], axis=1)
    add_s, add_o = kt[:, :, :HEAD_DIM], qs[:, :, :HEAD_DIM]

    onw = onw_ref[...]
    states = [s_ref[hd] for hd in range(heads)]
    for c in range(chunks):
        r0 = c * CHUNK
        for hd in range(heads):
            i = hd * chunks + c
            lanes = slice(hd * HEAD_DIM, (hd + 1) * HEAD_DIM)
            prod = _dot(lhs[i], states[hd])
            states[hd] = states[hd] * carry[i] - prod[:HEAD_DIM] + add_s[i]
            o = prod[HEAD_DIM:] + add_o[i]
            zg = _silu(z_ref[r0:r0 + CHUNK, lanes].astype(F32))
            o = o * lax.rsqrt(jnp.mean(o * o, axis=-1, keepdims=True) + EPS) * onw
            o_ref[r0:r0 + CHUNK, lanes] = (o * zg).astype(o_ref.dtype)
    for hd in range(heads):
        s_ref[hd] = states[hd]


def _gdn(proj, gates_t, conv_w, a_log, dt_bias, out_norm_w, batch, seq):
    rows, heads = GDN_ROWS, GDN_HEADS
    assert seq % rows == 0 and HEADS % heads == 0
    nt, groups, width = seq // rows, HEADS // heads, heads * HEAD_DIM
    blk = lambda off: pl.BlockSpec((rows, width), lambda b, h, t: (b * nt + t, off * groups + h))
    cw = lambda off: pl.BlockSpec((CONV_K, width), lambda b, h, t: (0, off * groups + h))
    gate = lambda off: pl.BlockSpec((None, heads, 1, rows), lambda b, h, t: (b, off * groups + h, 0, t))
    smem = pl.BlockSpec(memory_space=pltpu.SMEM)
    return pl.pallas_call(
        _gdn_body,
        name="gdn",
        grid=(batch, groups, nt),
        in_specs=[smem, smem,
                  blk(0), blk(1), blk(2), blk(3),
                  cw(0), cw(1), cw(2),
                  gate(0), gate(1),
                  pl.BlockSpec((1, HEAD_DIM), lambda b, h, t: (0, 0))],
        out_specs=pl.BlockSpec((rows, width), lambda b, h, t: (b * nt + t, h)),
        out_shape=jax.ShapeDtypeStruct((batch * seq, INNER), BF16),
        scratch_shapes=[pltpu.VMEM((rows + SUBLANES, width), F32)] * 3
                       + [pltpu.VMEM((heads, HEAD_DIM, HEAD_DIM), F32)],
        compiler_params=pltpu.CompilerParams(
            dimension_semantics=("parallel", "parallel", "arbitrary"), vmem_limit_bytes=VMEM_LIMIT),
    )(a_log, dt_bias, proj, proj, proj, proj, conv_w, conv_w, conv_w, gates_t, gates_t,
      out_norm_w.reshape(1, HEAD_DIM))


def _attn_body(q_ref, k_ref, v_ref, z_ref, qnw_ref, knw_ref, bias_ref, o_ref, kbuf_ref, vbuf_ref, s_ref):
    rows, width = q_ref.shape
    step = pl.program_id(2)

    @pl.when(step == 0)
    def _():
        kbuf_ref[0:LEFT, :] = jnp.zeros((LEFT, width), BF16)
        vbuf_ref[0:LEFT, :] = jnp.zeros((LEFT, width), BF16)

    def rms(x, w):
        return x * lax.rsqrt(jnp.mean(x * x, axis=-1, keepdims=True) + EPS) * w

    vbuf_ref[LEFT:LEFT + rows, :] = v_ref[...].astype(BF16)
    first_key = step * rows - LEFT
    col = lax.broadcasted_iota(jnp.int32, (CHUNK, BAND_PAD), 1)
    for hd in range(width // HEAD_DIM):
        lanes = slice(hd * HEAD_DIM, (hd + 1) * HEAD_DIM)
        q = rms(q_ref[:, lanes].astype(F32), qnw_ref[...]) * (HEAD_DIM ** -0.5)
        kbuf_ref[LEFT:LEFT + rows, lanes] = rms(k_ref[:, lanes].astype(F32), knw_ref[...]).astype(BF16)
        s_ref[hd] = _dot_nt(q, kbuf_ref[:, lanes])
        for c in range(rows // CHUNK):
            r0 = c * CHUNK
            lo = (c // 2) * LANES
            s = s_ref[hd, r0:r0 + CHUNK, lo:lo + BAND_PAD] + bias_ref[hd, c % 2]
            s = jnp.where(first_key + lo + col >= 0, s, NEG)
            p = jnp.exp(s - jnp.max(s, axis=-1, keepdims=True))
            denom = jnp.sum(p, axis=-1, keepdims=True)
            o = _dot(p, vbuf_ref[lo:lo + BAND_PAD, lanes]) / denom
            o = o * _silu(z_ref[r0:r0 + CHUNK, lanes].astype(F32))
            o_ref[r0:r0 + CHUNK, lanes] = o.astype(o_ref.dtype)

    kbuf_ref[0:LEFT, :] = kbuf_ref[rows:rows + LEFT, :]
    vbuf_ref[0:LEFT, :] = vbuf_ref[rows:rows + LEFT, :]


def _attn(proj, q_norm_w, k_norm_w, bias_tab, batch, seq):
    rows, heads = ATT_ROWS, ATT_HEADS
    assert seq % rows == 0 and rows >= LEFT and rows % LANES == 0 and HEADS % heads == 0
    nt, groups, width = seq // rows, HEADS // heads, heads * HEAD_DIM
    blk = lambda off: pl.BlockSpec((rows, width), lambda b, h, t: (b * nt + t, off * groups + h))
    vec = pl.BlockSpec((1, HEAD_DIM), lambda b, h, t: (0, 0))
    return pl.pallas_call(
        _attn_body,
        name="band_attn",
        grid=(batch, groups, nt),
        in_specs=[blk(0), blk(1), blk(2), blk(3), vec, vec,
                  pl.BlockSpec((heads, 2, CHUNK, BAND_PAD), lambda b, h, t: (h, 0, 0, 0))],
        out_specs=pl.BlockSpec((rows, width), lambda b, h, t: (b * nt + t, h)),
        out_shape=jax.ShapeDtypeStruct((batch * seq, INNER), BF16),
        scratch_shapes=[pltpu.VMEM((LEFT + rows, width), BF16),
                        pltpu.VMEM((LEFT + rows, width), BF16),
                        pltpu.VMEM((heads, rows, LEFT + rows), F32)],
        compiler_params=pltpu.CompilerParams(
            dimension_semantics=("parallel", "parallel", "arbitrary"), vmem_limit_bytes=VMEM_LIMIT),
    )(proj, proj, proj, proj, q_norm_w.reshape(1, HEAD_DIM), k_norm_w.reshape(1, HEAD_DIM), bias_tab)


def _bias_table(rel_bias):
    t = jnp.arange(BAND + CHUNK - 1)
    idx = jnp.clip(LEFT + CHUNK - 1 - t, -REL_CLIP, REL_CLIP) + REL_CLIP
    diag = rel_bias.astype(F32)[:, idx]
    band = jnp.stack([diag[:, CHUNK - 1 - r:CHUNK - 1 - r + BAND] for r in range(CHUNK)], axis=1)
    pad = jnp.full((HEADS, CHUNK, CHUNK), NEG, F32)
    even = jnp.concatenate([band, pad], axis=-1)
    odd = jnp.concatenate([pad, band], axis=-1)
    return jnp.stack([even, odd], axis=1)


def kernel(x, norm_w, a_w_in, a_conv_w, a_a_log, a_dt_bias, a_out_norm_w, a_w_out,
           b_w_in, b_q_norm_w, b_k_norm_w, b_rel_bias, b_w_out):
    batch, seq, d = x.shape
    h0 = x.reshape(batch * seq, d)

    w_in = a_w_in[0]
    w_tail = jnp.pad(w_in[:, 4 * INNER:], ((0, 0), (0, LANES - 2 * HEADS))).astype(BF16)
    proj, tail = _rms_proj(h0, norm_w[0], w_in.astype(BF16), 4 * INNER, w_tail)
    gates_t = tail[:, :2 * HEADS].reshape(batch, seq, 2 * HEADS).transpose(0, 2, 1)
    gates_t = gates_t.reshape(batch, 2 * HEADS, 1, seq)
    mixed = _gdn(proj, gates_t, a_conv_w[0], a_a_log[0], a_dt_bias[0], a_out_norm_w[0], batch, seq)
    h1 = _out_proj(mixed, a_w_out[0].astype(BF16), h0)

    proj = _rms_proj(h1, norm_w[1], b_w_in[0].astype(BF16), 4 * INNER)
    mixed = _attn(proj, b_q_norm_w[0], b_k_norm_w[0], _bias_table(b_rel_bias[0]), batch, seq)
    h2 = _out_proj(mixed, b_w_out[0].astype(BF16), h1)
    return h2.reshape(batch, seq, d)
```

```python
import functools

import jax
import jax.numpy as jnp
from jax import lax
from jax.experimental import pallas as pl
from jax.experimental.pallas import tpu as pltpu

F32 = jnp.float32
BF16 = jnp.bfloat16

EPS = 1e-6
CHUNK = 64
HEADS = 16
HEAD_DIM = 128
INNER = HEADS * HEAD_DIM
CONV_K = 4
LEFT_CHUNKS = 8
LEFT = LEFT_CHUNKS * CHUNK
BAND = LEFT + CHUNK
REL_CLIP = 256
NEG = -1e30
LOG2E = 1.4426950408889634

LANES = 128
SUBLANES = 8
VMEM_LIMIT = 48 * 1024 * 1024

GDN_ROWS = 512
GDN_HEADS = 4
GDN_SPLIT = 1
ATT_ROWS = 512
ATT_HEADS = 4
BAND_PAD = BAND + CHUNK


def _dot(a, b):
    return jnp.dot(a.astype(BF16), b.astype(BF16), preferred_element_type=F32)


def _dot_nt(a, b):
    return lax.dot_general(a.astype(BF16), b.astype(BF16), (((1,), (1,)), ((), ())),
                           preferred_element_type=F32)


def _silu(x):
    return x * jax.nn.sigmoid(x)


def _softplus(x):
    return jnp.maximum(x, 0.0) + jnp.log1p(jnp.exp(-jnp.abs(x)))


def _rms_proj_body(x_ref, nw_ref, w_ref, *rest, has_tail):
    if has_tail:
        wt_ref, o_ref, ot_ref, hn_ref = rest
    else:
        o_ref, hn_ref = rest

    @pl.when(pl.program_id(1) == 0)
    def _():
        x = x_ref[...]
        ms = jnp.mean(x * x, axis=-1, keepdims=True)
        hn = (x * lax.rsqrt(ms + EPS) * nw_ref[...]).astype(BF16)
        hn_ref[...] = hn
        if has_tail:
            ot_ref[...] = jnp.dot(hn, wt_ref[...], preferred_element_type=F32)

    o_ref[...] = jnp.dot(hn_ref[...], w_ref[...], preferred_element_type=F32).astype(o_ref.dtype)


def _rms_proj(x, nw, w, n, w_tail=None, *, tm=1024, tn=1024, out_dtype=BF16):
    m, d = x.shape
    assert m % tm == 0 and n % tn == 0 and n <= w.shape[1]
    has_tail = w_tail is not None
    in_specs = [pl.BlockSpec((tm, d), lambda i, j: (i, 0)),
                pl.BlockSpec((1, d), lambda i, j: (0, 0)),
                pl.BlockSpec((d, tn), lambda i, j: (0, j))]
    out_specs = [pl.BlockSpec((tm, tn), lambda i, j: (i, j))]
    out_shape = [jax.ShapeDtypeStruct((m, n), out_dtype)]
    args = [x, nw.reshape(1, d), w]
    if has_tail:
        nt = w_tail.shape[1]
        in_specs.append(pl.BlockSpec((d, nt), lambda i, j: (0, 0)))
        out_specs.append(pl.BlockSpec((tm, nt), lambda i, j: (i, 0)))
        out_shape.append(jax.ShapeDtypeStruct((m, nt), F32))
        args.append(w_tail)
    res = pl.pallas_call(
        functools.partial(_rms_proj_body, has_tail=has_tail),
        name="rms_proj_tail" if has_tail else "rms_proj",
        grid=(m // tm, n // tn),
        in_specs=in_specs, out_specs=out_specs, out_shape=out_shape,
        scratch_shapes=[pltpu.VMEM((tm, d), BF16)],
        compiler_params=pltpu.CompilerParams(
            dimension_semantics=("parallel", "arbitrary"), vmem_limit_bytes=VMEM_LIMIT),
    )(*args)
    return res if has_tail else res[0]


def _out_proj_body(a_ref, w_ref, r_ref, o_ref):
    o_ref[...] = r_ref[...] + jnp.dot(a_ref[...], w_ref[...], preferred_element_type=F32)


def _out_proj(a, w, res, *, tm=512):
    m, k = a.shape
    n = w.shape[1]
    assert m % tm == 0
    return pl.pallas_call(
        _out_proj_body,
        name="out_proj",
        grid=(m // tm,),
        in_specs=[pl.BlockSpec((tm, k), lambda i: (i, 0)),
                  pl.BlockSpec((k, n), lambda i: (0, 0)),
                  pl.BlockSpec((tm, n), lambda i: (i, 0))],
        out_specs=pl.BlockSpec((tm, n), lambda i: (i, 0)),
        out_shape=jax.ShapeDtypeStruct((m, n), F32),
        compiler_params=pltpu.CompilerParams(
            dimension_semantics=("parallel",), vmem_limit_bytes=VMEM_LIMIT),
    )(a, w, res)


def _bdot(a, b):
    return jnp.einsum("cij,cjk->cik", a.astype(BF16), b.astype(BF16), preferred_element_type=F32)


def _bdot_nt(a, b):
    return jnp.einsum("cid,cjd->cij", a.astype(BF16), b.astype(BF16), preferred_element_type=F32)


def _bdot_tn(a, b):
    return jnp.einsum("cjd,cjn->cdn", a.astype(BF16), b.astype(BF16), preferred_element_type=F32)


def _unit_lower_inverse(low, row, col):
    eye = (row == col).astype(F32)
    base = SUBLANES
    ld = jnp.where((row // base) == (col // base), low, 0.0)
    l2 = _bdot(ld, ld)
    l4 = _bdot(l2, l2)
    x = eye - ld
    x = x + _bdot(x, l2)
    x = x + _bdot(x, l4)
    size = base
    while size < CHUNK:
        pair = 2 * size
        off = ((row // pair) == (col // pair)) & ((row % pair) >= size) & ((col % pair) < size)
        c = jnp.where(off, low, 0.0)
        x = x - _bdot(x, _bdot(c, x))
        size = pair
    return x


def _gdn_body(alog_ref, dtb_ref, q_ref, k_ref, v_ref, z_ref, cwq_ref, cwk_ref, cwv_ref,
              a_ref, b_ref, onw_ref, o_ref, xq_ref, xk_ref, xv_ref, s_ref):
    rows, width = q_ref.shape
    heads = width // HEAD_DIM
    chunks = rows // CHUNK
    head0 = pl.program_id(1) * heads
    step = pl.program_id(2)

    @pl.when(step == 0)
    def _():
        zeros = jnp.zeros((SUBLANES, width), F32)
        xq_ref[0:SUBLANES, :] = zeros
        xk_ref[0:SUBLANES, :] = zeros
        xv_ref[0:SUBLANES, :] = zeros
        s_ref[...] = jnp.zeros_like(s_ref)

    for x_ref, buf_ref in ((q_ref, xq_ref), (k_ref, xk_ref), (v_ref, xv_ref)):
        buf_ref[SUBLANES:SUBLANES + rows, :] = x_ref[...].astype(F32)

    pos = lax.broadcasted_iota(jnp.int32, (SUBLANES, rows), 1) % CHUNK
    sub = lax.broadcasted_iota(jnp.int32, (SUBLANES, HEAD_DIM), 0)
    row = lax.broadcasted_iota(jnp.int32, (CHUNK, CHUNK), 0)
    col = lax.broadcasted_iota(jnp.int32, (CHUNK, CHUNK), 1)

    def gates(hd):
        a_in = jnp.broadcast_to(a_ref[hd], (SUBLANES, rows))
        b_in = jnp.broadcast_to(b_ref[hd], (SUBLANES, rows))
        neg_rate = -jnp.exp(jnp.full((SUBLANES, rows), alog_ref[head0 + hd], F32))
        gc = neg_rate * _softplus(a_in + dtb_ref[head0 + hd])
        beta = jax.nn.sigmoid(b_in)
        shift = 1
        while shift < CHUNK:
            gc = gc + jnp.where(pos >= shift, pltpu.roll(gc, shift, 1), 0.0)
            shift *= 2
        return (jnp.transpose(beta[0:1, :]).reshape(chunks, CHUNK, 1),
                jnp.transpose(gc[0:1, :]).reshape(chunks, CHUNK, 1),
                jnp.stack([gc[0:1, c * CHUNK:(c + 1) * CHUNK] for c in range(chunks)]))

    def conv_silu(buf_ref, cw_ref, hd, c0, c1):
        lanes = slice(hd * HEAD_DIM, (hd + 1) * HEAD_DIM)
        taps = [jnp.broadcast_to(cw_ref[j:j + 1, lanes], (SUBLANES, HEAD_DIM)) for j in range(CONV_K)]
        tiles = []
        for c in range(c0, c1):
            ext = buf_ref[c * CHUNK:(c + 1) * CHUNK + SUBLANES, lanes]
            ext = ext.reshape(CHUNK // SUBLANES + 1, SUBLANES, HEAD_DIM)
            y = ext[1:] * taps[CONV_K - 1]
            for shift in range(1, CONV_K):
                rot = pltpu.roll(ext, shift, 1)
                y = y + jnp.where(sub < shift, rot[:-1], rot[1:]) * taps[CONV_K - 1 - shift]
            tiles.append(_silu(y).reshape(CHUNK, HEAD_DIM))
        return jnp.stack(tiles)

    head_gates = [gates(hd) for hd in range(heads)]

    def chunk_local(c0, c1):
        parts = {name: [] for name in ("q", "k", "v", "beta", "gc", "gcr")}
        for hd in range(heads):
            beta, gc, gcr = head_gates[hd]
            q = conv_silu(xq_ref, cwq_ref, hd, c0, c1)
            k = conv_silu(xk_ref, cwk_ref, hd, c0, c1)
            parts["q"].append(q * lax.rsqrt(jnp.sum(q * q, axis=-1, keepdims=True) + EPS) * (HEAD_DIM ** -0.5))
            parts["k"].append(k * lax.rsqrt(jnp.sum(k * k, axis=-1, keepdims=True) + EPS))
            parts["v"].append(conv_silu(xv_ref, cwv_ref, hd, c0, c1))
            parts["beta"].append(beta[c0:c1])
            parts["gc"].append(gc[c0:c1])
            parts["gcr"].append(gcr[c0:c1])
        q, k, v, beta, gc, gcr = (jnp.concatenate(parts[n], axis=0) for n in ("q", "k", "v", "beta", "gc", "gcr"))
        gcl = gc[:, CHUNK - 1:CHUNK, :]
        decay = jnp.exp(jnp.where(row >= col, gc - gcr, -jnp.inf))
        egc = jnp.exp(gc)
        kb = k * beta
        low = jnp.where(row > col, _bdot_nt(kb, k) * decay, 0.0)
        inv = _unit_lower_inverse(low, row, col)
        sol = _bdot(inv, jnp.concatenate([v * beta, kb * egc], axis=-1))
        qk = _bdot_nt(q, k) * decay
        k_dec = k * jnp.exp(gcl - gc)
        kt = _bdot_tn(k_dec, sol)
        qs = _bdot(qk, sol)
        lhs = jnp.concatenate([kt[:, :, HEAD_DIM:], q * egc - qs[:, :, HEAD_DIM:]], axis=1)
        return lhs, kt[:, :, :HEAD_DIM], qs[:, :, :HEAD_DIM], jnp.exp(gcl)

    per = chunks // GDN_SPLIT
    local = [chunk_local(g * per, (g + 1) * per) for g in range(GDN_SPLIT)]
    for buf_ref in (xq_ref, xk_ref, xv_ref):
        buf_ref[0:SUBLANES, :] = buf_ref[rows:rows + SUBLANES, :]

    onw = onw_ref[...]
    states = [s_ref[hd] for hd in range(heads)]
    for c in range(chunks):
        r0 = c * CHUNK
        for hd in range(heads):
            lhs, add_s, add_o, carry = local[c // per]
            i = hd * per + c % per
            lanes = slice(hd * HEAD_DIM, (hd + 1) * HEAD_DIM)
            prod = _dot(lhs[i], states[hd])
            states[hd] = states[hd] * carry[i] - prod[:HEAD_DIM] + add_s[i]
            o = prod[HEAD_DIM:] + add_o[i]
            zg = _silu(z_ref[r0:r0 + CHUNK, lanes].astype(F32))
            o = o * lax.rsqrt(jnp.mean(o * o, axis=-1, keepdims=True) + EPS) * onw
            o_ref[r0:r0 + CHUNK, lanes] = (o * zg).astype(o_ref.dtype)
    for hd in range(heads):
        s_ref[hd] = states[hd]


def _gdn(proj, gates_t, conv_w, a_log, dt_bias, out_norm_w, batch, seq):
    rows, heads = GDN_ROWS, GDN_HEADS
    assert seq % rows == 0 and HEADS % heads == 0
    nt, groups, width = seq // rows, HEADS // heads, heads * HEAD_DIM
    blk = lambda off: pl.BlockSpec((rows, width), lambda b, h, t: (b * nt + t, off * groups + h))
    cw = lambda off: pl.BlockSpec((CONV_K, width), lambda b, h, t: (0, off * groups + h))
    gate = lambda off: pl.BlockSpec((None, heads, 1, rows), lambda b, h, t: (b, off * groups + h, 0, t))
    smem = pl.BlockSpec(memory_space=pltpu.SMEM)
    return pl.pallas_call(
        _gdn_body,
        name="gdn",
        grid=(batch, groups, nt),
        in_specs=[smem, smem,
                  blk(0), blk(1), blk(2), blk(3),
                  cw(0), cw(1), cw(2),
                  gate(0), gate(1),
                  pl.BlockSpec((1, HEAD_DIM), lambda b, h, t: (0, 0))],
        out_specs=pl.BlockSpec((rows, width), lambda b, h, t: (b * nt + t, h)),
        out_shape=jax.ShapeDtypeStruct((batch * seq, INNER), BF16),
        scratch_shapes=[pltpu.VMEM((rows + SUBLANES, width), F32)] * 3
                       + [pltpu.VMEM((heads, HEAD_DIM, HEAD_DIM), F32)],
        compiler_params=pltpu.CompilerParams(
            dimension_semantics=("parallel", "parallel", "arbitrary"), vmem_limit_bytes=VMEM_LIMIT),
    )(a_log, dt_bias, proj, proj, proj, proj, conv_w, conv_w, conv_w, gates_t, gates_t,
      out_norm_w.reshape(1, HEAD_DIM))


def _attn_body(q_ref, k_ref, v_ref, z_ref, qnw_ref, knw_ref, bias_ref, o_ref, kbuf_ref, vbuf_ref, s_ref):
    rows, width = q_ref.shape
    step = pl.program_id(2)

    @pl.when(step == 0)
    def _():
        kbuf_ref[0:LEFT, :] = jnp.zeros((LEFT, width), BF16)
        vbuf_ref[0:LEFT, :] = jnp.zeros((LEFT, width), BF16)

    def rms(x, w):
        return x * lax.rsqrt(jnp.mean(x * x, axis=-1, keepdims=True) + EPS) * w

    vbuf_ref[LEFT:LEFT + rows, :] = v_ref[...].astype(BF16)
    first_key = step * rows - LEFT
    col = lax.broadcasted_iota(jnp.int32, (1, BAND_PAD), 1)
    for hd in range(width // HEAD_DIM):
        lanes = slice(hd * HEAD_DIM, (hd + 1) * HEAD_DIM)
        q = rms(q_ref[:, lanes].astype(F32), qnw_ref[...]) * (HEAD_DIM ** -0.5 * LOG2E)
        kbuf_ref[LEFT:LEFT + rows, lanes] = rms(k_ref[:, lanes].astype(F32), knw_ref[...]).astype(BF16)
        s_ref[hd] = _dot_nt(q, kbuf_ref[:, lanes])
        for c in range(rows // CHUNK):
            r0 = c * CHUNK
            lo = (c // 2) * LANES
            before_start = jnp.where(first_key + lo + col >= 0, 0.0, NEG)
            s = s_ref[hd, r0:r0 + CHUNK, lo:lo + BAND_PAD] + bias_ref[hd, c % 2] + before_start
            p = jnp.exp2(s - jnp.max(s, axis=-1, keepdims=True))
            denom = jnp.sum(p, axis=-1, keepdims=True)
            o = _dot(p, vbuf_ref[lo:lo + BAND_PAD, lanes]) / denom
            o = o * _silu(z_ref[r0:r0 + CHUNK, lanes].astype(F32))
            o_ref[r0:r0 + CHUNK, lanes] = o.astype(o_ref.dtype)

    kbuf_ref[0:LEFT, :] = kbuf_ref[rows:rows + LEFT, :]
    vbuf_ref[0:LEFT, :] = vbuf_ref[rows:rows + LEFT, :]


def _attn(proj, q_norm_w, k_norm_w, bias_tab, batch, seq):
    rows, heads = ATT_ROWS, ATT_HEADS
    assert seq % rows == 0 and rows >= LEFT and rows % LANES == 0 and HEADS % heads == 0
    nt, groups, width = seq // rows, HEADS // heads, heads * HEAD_DIM
    blk = lambda off: pl.BlockSpec((rows, width), lambda b, h, t: (b * nt + t, off * groups + h))
    vec = pl.BlockSpec((1, HEAD_DIM), lambda b, h, t: (0, 0))
    return pl.pallas_call(
        _attn_body,
        name="band_attn",
        grid=(batch, groups, nt),
        in_specs=[blk(0), blk(1), blk(2), blk(3), vec, vec,
                  pl.BlockSpec((heads, 2, CHUNK, BAND_PAD), lambda b, h, t: (h, 0, 0, 0))],
        out_specs=pl.BlockSpec((rows, width), lambda b, h, t: (b * nt + t, h)),
        out_shape=jax.ShapeDtypeStruct((batch * seq, INNER), BF16),
        scratch_shapes=[pltpu.VMEM((LEFT + rows, width), BF16),
                        pltpu.VMEM((LEFT + rows, width), BF16),
                        pltpu.VMEM((heads, rows, LEFT + rows), F32)],
        compiler_params=pltpu.CompilerParams(
            dimension_semantics=("parallel", "parallel", "arbitrary"), vmem_limit_bytes=VMEM_LIMIT),
    )(proj, proj, proj, proj, q_norm_w.reshape(1, HEAD_DIM), k_norm_w.reshape(1, HEAD_DIM), bias_tab)


def _bias_table(rel_bias):
    t = jnp.arange(BAND + CHUNK - 1)
    idx = jnp.clip(LEFT + CHUNK - 1 - t, -REL_CLIP, REL_CLIP) + REL_CLIP
    diag = rel_bias.astype(F32)[:, idx] * LOG2E
    band = jnp.stack([diag[:, CHUNK - 1 - r:CHUNK - 1 - r + BAND] for r in range(CHUNK)], axis=1)
    pad = jnp.full((HEADS, CHUNK, CHUNK), NEG, F32)
    even = jnp.concatenate([band, pad], axis=-1)
    odd = jnp.concatenate([pad, band], axis=-1)
    return jnp.stack([even, odd], axis=1)


def kernel(x, norm_w, a_w_in, a_conv_w, a_a_log, a_dt_bias, a_out_norm_w, a_w_out,
           b_w_in, b_q_norm_w, b_k_norm_w, b_rel_bias, b_w_out):
    batch, seq, d = x.shape
    h0 = x.reshape(batch * seq, d)

    w_in = a_w_in[0]
    w_tail = jnp.pad(w_in[:, 4 * INNER:], ((0, 0), (0, LANES - 2 * HEADS))).astype(BF16)
    proj, tail = _rms_proj(h0, norm_w[0], w_in.astype(BF16), 4 * INNER, w_tail)
    gates_t = tail[:, :2 * HEADS].reshape(batch, seq, 2 * HEADS).transpose(0, 2, 1)
    gates_t = gates_t.reshape(batch, 2 * HEADS, 1, seq)
    mixed = _gdn(proj, gates_t, a_conv_w[0], a_a_log[0], a_dt_bias[0], a_out_norm_w[0], batch, seq)
    h1 = _out_proj(mixed, a_w_out[0].astype(BF16), h0)

    proj = _rms_proj(h1, norm_w[1], b_w_in[0].astype(BF16), 4 * INNER)
    mixed = _attn(proj, b_q_norm_w[0], b_k_norm_w[0], _bias_table(b_rel_bias[0]), batch, seq)
    h2 = _out_proj(mixed, b_w_out[0].astype(BF16), h1)
    return h2.reshape(batch, seq, d)
```

```python
import functools

import jax
import jax.numpy as jnp
from jax import lax
from jax.experimental import pallas as pl
from jax.experimental.pallas import tpu as pltpu

F32 = jnp.float32
BF16 = jnp.bfloat16

EPS = 1e-6
CHUNK = 64
HEADS = 16
HEAD_DIM = 128
INNER = HEADS * HEAD_DIM
CONV_K = 4
LEFT_CHUNKS = 8
LEFT = LEFT_CHUNKS * CHUNK
BAND = LEFT + CHUNK
REL_CLIP = 256
NEG = -1e30
LOG2E = 1.4426950408889634

LANES = 128
SUBLANES = 8
VMEM_LIMIT = 54 * 1024 * 1024

GDN_ROWS = 512
GDN_HEADS = 4
GDN_SPLIT = 1
ATT_ROWS = 512
ATT_HEADS = 4
BAND_PAD = BAND + CHUNK


def _dot(a, b):
    return jnp.dot(a.astype(BF16), b.astype(BF16), preferred_element_type=F32)


def _dot_nt(a, b):
    return lax.dot_general(a.astype(BF16), b.astype(BF16), (((1,), (1,)), ((), ())),
                           preferred_element_type=F32)


def _silu(x):
    return x * jax.nn.sigmoid(x)


def _softplus(x):
    return jnp.maximum(x, 0.0) + jnp.log1p(jnp.exp(-jnp.abs(x)))


def _rms_proj_body(x_ref, nw_ref, w_ref, *rest, has_tail):
    if has_tail:
        wt_ref, o_ref, ot_ref, hn_ref = rest
    else:
        o_ref, hn_ref = rest

    @pl.when(pl.program_id(1) == 0)
    def _():
        x = x_ref[...]
        ms = jnp.mean(x * x, axis=-1, keepdims=True)
        hn = (x * lax.rsqrt(ms + EPS) * nw_ref[...]).astype(BF16)
        hn_ref[...] = hn
        if has_tail:
            ot_ref[...] = jnp.dot(hn, wt_ref[...], preferred_element_type=F32)

    o_ref[...] = jnp.dot(hn_ref[...], w_ref[...], preferred_element_type=F32).astype(o_ref.dtype)


def _rms_proj(x, nw, w, n, w_tail=None, *, tm=1024, tn=2048, out_dtype=BF16):
    m, d = x.shape
    assert m % tm == 0 and n % tn == 0 and n <= w.shape[1]
    has_tail = w_tail is not None
    in_specs = [pl.BlockSpec((tm, d), lambda i, j: (i, 0)),
                pl.BlockSpec((1, d), lambda i, j: (0, 0)),
                pl.BlockSpec((d, tn), lambda i, j: (0, j))]
    out_specs = [pl.BlockSpec((tm, tn), lambda i, j: (i, j))]
    out_shape = [jax.ShapeDtypeStruct((m, n), out_dtype)]
    args = [x, nw.reshape(1, d), w]
    if has_tail:
        nt = w_tail.shape[1]
        in_specs.append(pl.BlockSpec((d, nt), lambda i, j: (0, 0)))
        out_specs.append(pl.BlockSpec((tm, nt), lambda i, j: (i, 0)))
        out_shape.append(jax.ShapeDtypeStruct((m, nt), F32))
        args.append(w_tail)
    res = pl.pallas_call(
        functools.partial(_rms_proj_body, has_tail=has_tail),
        name="rms_proj_tail" if has_tail else "rms_proj",
        grid=(m // tm, n // tn),
        in_specs=in_specs, out_specs=out_specs, out_shape=out_shape,
        scratch_shapes=[pltpu.VMEM((tm, d), BF16)],
        compiler_params=pltpu.CompilerParams(
            dimension_semantics=("parallel", "arbitrary"), vmem_limit_bytes=VMEM_LIMIT),
    )(*args)
    return res if has_tail else res[0]


def _out_proj_body(a_ref, w_ref, r_ref, o_ref):
    o_ref[...] = r_ref[...] + jnp.dot(a_ref[...], w_ref[...], preferred_element_type=F32)


def _out_proj(a, w, res, *, tm=512):
    m, k = a.shape
    n = w.shape[1]
    assert m % tm == 0
    return pl.pallas_call(
        _out_proj_body,
        name="out_proj",
        grid=(m // tm,),
        in_specs=[pl.BlockSpec((tm, k), lambda i: (i, 0)),
                  pl.BlockSpec((k, n), lambda i: (0, 0)),
                  pl.BlockSpec((tm, n), lambda i: (i, 0))],
        out_specs=pl.BlockSpec((tm, n), lambda i: (i, 0)),
        out_shape=jax.ShapeDtypeStruct((m, n), F32),
        compiler_params=pltpu.CompilerParams(
            dimension_semantics=("parallel",), vmem_limit_bytes=VMEM_LIMIT),
    )(a, w, res)


def _bdot(a, b):
    return jnp.einsum("cij,cjk->cik", a.astype(BF16), b.astype(BF16), preferred_element_type=F32)


def _bdot_nt(a, b):
    return jnp.einsum("cid,cjd->cij", a.astype(BF16), b.astype(BF16), preferred_element_type=F32)


def _bdot_tn(a, b):
    return jnp.einsum("cjd,cjn->cdn", a.astype(BF16), b.astype(BF16), preferred_element_type=F32)


def _unit_lower_inverse(low, row, col):
    eye = (row == col).astype(F32)
    base = SUBLANES
    ld = jnp.where((row // base) == (col // base), low, 0.0)
    l2 = _bdot(ld, ld)
    l4 = _bdot(l2, l2)
    x = eye - ld
    x = x + _bdot(x, l2)
    x = x + _bdot(x, l4)
    size = base
    low_b = low.astype(BF16)
    while size < CHUNK:
        pair = 2 * size
        off = ((row // pair) == (col // pair)) & ((row % pair) >= size) & ((col % pair) < size)
        c = jnp.where(off, low_b, jnp.zeros_like(low_b))
        x = x - _bdot(x, _bdot(c, x))
        size = pair
    return x


def _gdn_body(alog_ref, dtb_ref, q_ref, k_ref, v_ref, z_ref, cwq_ref, cwk_ref, cwv_ref,
              a_ref, b_ref, onw_ref, o_ref, xq_ref, xk_ref, xv_ref, s_ref):
    rows, width = q_ref.shape
    heads = width // HEAD_DIM
    chunks = rows // CHUNK
    head0 = pl.program_id(1) * heads
    step = pl.program_id(2)

    @pl.when(step == 0)
    def _():
        zeros = jnp.zeros((SUBLANES, width), F32)
        xq_ref[0:SUBLANES, :] = zeros
        xk_ref[0:SUBLANES, :] = zeros
        xv_ref[0:SUBLANES, :] = zeros
        s_ref[...] = jnp.zeros_like(s_ref)

    for x_ref, buf_ref in ((q_ref, xq_ref), (k_ref, xk_ref), (v_ref, xv_ref)):
        buf_ref[SUBLANES:SUBLANES + rows, :] = x_ref[...].astype(F32)

    pos = lax.broadcasted_iota(jnp.int32, (SUBLANES, rows), 1) % CHUNK
    sub = lax.broadcasted_iota(jnp.int32, (SUBLANES, HEAD_DIM), 0)
    row = lax.broadcasted_iota(jnp.int32, (CHUNK, CHUNK), 0)
    col = lax.broadcasted_iota(jnp.int32, (CHUNK, CHUNK), 1)

    def gates(hd):
        a_in = jnp.broadcast_to(a_ref[hd], (SUBLANES, rows))
        b_in = jnp.broadcast_to(b_ref[hd], (SUBLANES, rows))
        neg_rate = -jnp.exp(jnp.full((SUBLANES, rows), alog_ref[head0 + hd], F32))
        gc = neg_rate * _softplus(a_in + dtb_ref[head0 + hd])
        beta = jax.nn.sigmoid(b_in)
        shift = 1
        while shift < CHUNK:
            gc = gc + jnp.where(pos >= shift, pltpu.roll(gc, shift, 1), 0.0)
            shift *= 2
        return (jnp.transpose(beta[0:1, :]).reshape(chunks, CHUNK, 1),
                jnp.transpose(gc[0:1, :]).reshape(chunks, CHUNK, 1),
                jnp.stack([gc[0:1, c * CHUNK:(c + 1) * CHUNK] for c in range(chunks)]))

    def conv_silu(buf_ref, cw_ref, hd, c0, c1):
        lanes = slice(hd * HEAD_DIM, (hd + 1) * HEAD_DIM)
        taps = [jnp.broadcast_to(cw_ref[j:j + 1, lanes], (SUBLANES, HEAD_DIM)) for j in range(CONV_K)]
        tiles = []
        for c in range(c0, c1):
            ext = buf_ref[c * CHUNK:(c + 1) * CHUNK + SUBLANES, lanes]
            ext = ext.reshape(CHUNK // SUBLANES + 1, SUBLANES, HEAD_DIM)
            y = ext[1:] * taps[CONV_K - 1]
            for shift in range(1, CONV_K):
                rot = pltpu.roll(ext, shift, 1)
                y = y + jnp.where(sub < shift, rot[:-1], rot[1:]) * taps[CONV_K - 1 - shift]
            tiles.append(_silu(y).reshape(CHUNK, HEAD_DIM))
        return jnp.stack(tiles)

    head_gates = [gates(hd) for hd in range(heads)]

    def chunk_local(c0, c1):
        parts = {name: [] for name in ("q", "k", "v", "beta", "gc", "gcr")}
        for hd in range(heads):
            beta, gc, gcr = head_gates[hd]
            q = conv_silu(xq_ref, cwq_ref, hd, c0, c1)
            k = conv_silu(xk_ref, cwk_ref, hd, c0, c1)
            parts["q"].append(q * lax.rsqrt(jnp.sum(q * q, axis=-1, keepdims=True) + EPS) * (HEAD_DIM ** -0.5))
            parts["k"].append(k * lax.rsqrt(jnp.sum(k * k, axis=-1, keepdims=True) + EPS))
            parts["v"].append(conv_silu(xv_ref, cwv_ref, hd, c0, c1))
            parts["beta"].append(beta[c0:c1])
            parts["gc"].append(gc[c0:c1])
            parts["gcr"].append(gcr[c0:c1])
        q, k, v, beta, gc, gcr = (jnp.concatenate(parts[n], axis=0) for n in ("q", "k", "v", "beta", "gc", "gcr"))
        gcl = gc[:, CHUNK - 1:CHUNK, :]
        decay = jnp.exp(jnp.where(row >= col, gc - gcr, -jnp.inf))
        egc = jnp.exp(gc)
        kb = k * beta
        low = jnp.where(row > col, _bdot_nt(kb, k) * decay, 0.0)
        inv = _unit_lower_inverse(low, row, col)
        sol = _bdot(inv, jnp.concatenate([v * beta, kb * egc], axis=-1))
        qk = _bdot_nt(q, k) * decay
        k_dec = k * jnp.exp(gcl - gc)
        kt = _bdot_tn(k_dec, sol)
        qs = _bdot(qk, sol)
        lhs = jnp.concatenate([kt[:, :, HEAD_DIM:], q * egc - qs[:, :, HEAD_DIM:]], axis=1)
        return lhs, kt[:, :, :HEAD_DIM], qs[:, :, :HEAD_DIM], jnp.exp(gcl)

    per = chunks // GDN_SPLIT
    local = [chunk_local(g * per, (g + 1) * per) for g in range(GDN_SPLIT)]
    for buf_ref in (xq_ref, xk_ref, xv_ref):
        buf_ref[0:SUBLANES, :] = buf_ref[rows:rows + SUBLANES, :]

    onw = onw_ref[...]
    states = [s_ref[hd] for hd in range(heads)]
    for c in range(chunks):
        r0 = c * CHUNK
        for hd in range(heads):
            lhs, add_s, add_o, carry = local[c // per]
            i = hd * per + c % per
            lanes = slice(hd * HEAD_DIM, (hd + 1) * HEAD_DIM)
            prod = _dot(lhs[i], states[hd])
            states[hd] = states[hd] * carry[i] - prod[:HEAD_DIM] + add_s[i]
            o = prod[HEAD_DIM:] + add_o[i]
            zg = _silu(z_ref[r0:r0 + CHUNK, lanes].astype(F32))
            o = o * lax.rsqrt(jnp.mean(o * o, axis=-1, keepdims=True) + EPS) * onw
            o_ref[r0:r0 + CHUNK, lanes] = (o * zg).astype(o_ref.dtype)
    for hd in range(heads):
        s_ref[hd] = states[hd]


def _gdn(proj, gates_t, conv_w, a_log, dt_bias, out_norm_w, batch, seq):
    rows, heads = GDN_ROWS, GDN_HEADS
    assert seq % rows == 0 and HEADS % heads == 0
    nt, groups, width = seq // rows, HEADS // heads, heads * HEAD_DIM
    blk = lambda off: pl.BlockSpec((rows, width), lambda b, h, t: (b * nt + t, off * groups + h))
    cw = lambda off: pl.BlockSpec((CONV_K, width), lambda b, h, t: (0, off * groups + h))
    gate = lambda off: pl.BlockSpec((None, heads, 1, rows), lambda b, h, t: (b, off * groups + h, 0, t))
    smem = pl.BlockSpec(memory_space=pltpu.SMEM)
    return pl.pallas_call(
        _gdn_body,
        name="gdn",
        grid=(batch, groups, nt),
        in_specs=[smem, smem,
                  blk(0), blk(1), blk(2), blk(3),
                  cw(0), cw(1), cw(2),
                  gate(0), gate(1),
                  pl.BlockSpec((1, HEAD_DIM), lambda b, h, t: (0, 0))],
        out_specs=pl.BlockSpec((rows, width), lambda b, h, t: (b * nt + t, h)),
        out_shape=jax.ShapeDtypeStruct((batch * seq, INNER), BF16),
        scratch_shapes=[pltpu.VMEM((rows + SUBLANES, width), F32)] * 3
                       + [pltpu.VMEM((heads, HEAD_DIM, HEAD_DIM), F32)],
        compiler_params=pltpu.CompilerParams(
            dimension_semantics=("parallel", "parallel", "arbitrary"), vmem_limit_bytes=VMEM_LIMIT),
    )(a_log, dt_bias, proj, proj, proj, proj, conv_w, conv_w, conv_w, gates_t, gates_t,
      out_norm_w.reshape(1, HEAD_DIM))


def _attn_body(x_ref, qnw_ref, knw_ref, bias_ref, o_ref, kbuf_ref, vbuf_ref, s_ref):
    rows, width = o_ref.shape
    step = pl.program_id(2)

    @pl.when(step == 0)
    def _():
        kbuf_ref[0:LEFT, :] = jnp.zeros((LEFT, width), BF16)
        vbuf_ref[0:LEFT, :] = jnp.zeros((LEFT, width), BF16)

    def rms(x, w):
        return x * lax.rsqrt(jnp.mean(x * x, axis=-1, keepdims=True) + EPS) * w

    vbuf_ref[LEFT:LEFT + rows, :] = x_ref[:, 2 * width:3 * width]
    first_key = step * rows - LEFT
    col = lax.broadcasted_iota(jnp.int32, (1, BAND_PAD), 1)
    for hd in range(width // HEAD_DIM):
        lanes = slice(hd * HEAD_DIM, (hd + 1) * HEAD_DIM)
        q_lanes, k_lanes, z_lanes = (slice(part * width + lanes.start, part * width + lanes.stop)
                                     for part in (0, 1, 3))
        q = rms(x_ref[:, q_lanes].astype(F32), qnw_ref[...]) * (HEAD_DIM ** -0.5 * LOG2E)
        kbuf_ref[LEFT:LEFT + rows, lanes] = rms(x_ref[:, k_lanes].astype(F32), knw_ref[...]).astype(BF16)
        s_ref[hd] = _dot_nt(q, kbuf_ref[:, lanes])
        for c in range(rows // CHUNK):
            r0 = c * CHUNK
            lo = (c // 2) * LANES
            before_start = jnp.where(first_key + lo + col >= 0, 0.0, NEG)
            s = s_ref[hd, r0:r0 + CHUNK, lo:lo + BAND_PAD] + bias_ref[hd, c % 2] + before_start
            p = jnp.exp2(s - jnp.max(s, axis=-1, keepdims=True))
            denom = jnp.sum(p, axis=-1, keepdims=True)
            o = _dot(p, vbuf_ref[lo:lo + BAND_PAD, lanes]) / denom
            o = o * _silu(x_ref[r0:r0 + CHUNK, z_lanes].astype(F32))
            o_ref[r0:r0 + CHUNK, lanes] = o.astype(o_ref.dtype)

    kbuf_ref[0:LEFT, :] = kbuf_ref[rows:rows + LEFT, :]
    vbuf_ref[0:LEFT, :] = vbuf_ref[rows:rows + LEFT, :]


def _attn(proj, q_norm_w, k_norm_w, bias_tab, batch, seq):
    rows, heads = ATT_ROWS, ATT_HEADS
    assert seq % rows == 0 and rows >= LEFT and rows % LANES == 0 and HEADS % heads == 0
    nt, groups, width = seq // rows, HEADS // heads, heads * HEAD_DIM
    vec = pl.BlockSpec((1, HEAD_DIM), lambda b, h, t: (0, 0))
    return pl.pallas_call(
        _attn_body,
        name="band_attn",
        grid=(batch, groups, nt),
        in_specs=[pl.BlockSpec((rows, 4 * width), lambda b, h, t: (b * nt + t, h)), vec, vec,
                  pl.BlockSpec((heads, 2, CHUNK, BAND_PAD), lambda b, h, t: (h, 0, 0, 0))],
        out_specs=pl.BlockSpec((rows, width), lambda b, h, t: (b * nt + t, h)),
        out_shape=jax.ShapeDtypeStruct((batch * seq, INNER), BF16),
        scratch_shapes=[pltpu.VMEM((LEFT + rows, width), BF16),
                        pltpu.VMEM((LEFT + rows, width), BF16),
                        pltpu.VMEM((heads, rows, LEFT + rows), F32)],
        compiler_params=pltpu.CompilerParams(
            dimension_semantics=("parallel", "parallel", "arbitrary"), vmem_limit_bytes=VMEM_LIMIT),
    )(proj, q_norm_w.reshape(1, HEAD_DIM), k_norm_w.reshape(1, HEAD_DIM), bias_tab)


def _bias_table(rel_bias):
    t = jnp.arange(BAND + CHUNK - 1)
    idx = jnp.clip(LEFT + CHUNK - 1 - t, -REL_CLIP, REL_CLIP) + REL_CLIP
    diag = rel_bias.astype(F32)[:, idx] * LOG2E
    band = jnp.stack([diag[:, CHUNK - 1 - r:CHUNK - 1 - r + BAND] for r in range(CHUNK)], axis=1)
    pad = jnp.full((HEADS, CHUNK, CHUNK), NEG, F32)
    even = jnp.concatenate([band, pad], axis=-1)
    odd = jnp.concatenate([pad, band], axis=-1)
    return jnp.stack([even, odd], axis=1)


def kernel(x, norm_w, a_w_in, a_conv_w, a_a_log, a_dt_bias, a_out_norm_w, a_w_out,
           b_w_in, b_q_norm_w, b_k_norm_w, b_rel_bias, b_w_out):
    batch, seq, d = x.shape
    h0 = x.reshape(batch * seq, d)

    w_in = a_w_in[0]
    w_tail = jnp.pad(w_in[:, 4 * INNER:], ((0, 0), (0, LANES - 2 * HEADS))).astype(BF16)
    proj, tail = _rms_proj(h0, norm_w[0], w_in.astype(BF16), 4 * INNER, w_tail)
    gates_t = tail[:, :2 * HEADS].reshape(batch, seq, 2 * HEADS).transpose(0, 2, 1)
    gates_t = gates_t.reshape(batch, 2 * HEADS, 1, seq)
    mixed = _gdn(proj, gates_t, a_conv_w[0], a_a_log[0], a_dt_bias[0], a_out_norm_w[0], batch, seq)
    h1 = _out_proj(mixed, a_w_out[0].astype(BF16), h0)

    groups = HEADS // ATT_HEADS
    w_in = b_w_in[0].reshape(d, 4, groups, ATT_HEADS * HEAD_DIM).transpose(0, 2, 1, 3).reshape(d, 4 * INNER)
    proj = _rms_proj(h1, norm_w[1], w_in.astype(BF16), 4 * INNER)
    mixed = _attn(proj, b_q_norm_w[0], b_k_norm_w[0], _bias_table(b_rel_bias[0]), batch, seq)
    h2 = _out_proj(mixed, b_w_out[0].astype(BF16), h1)
    return h2.reshape(batch, seq, d)
```

```python
import functools

import jax
import jax.numpy as jnp
from jax import lax
from jax.experimental import pallas as pl
from jax.experimental.pallas import tpu as pltpu

F32 = jnp.float32
BF16 = jnp.bfloat16

EPS = 1e-6
CHUNK = 64
HEADS = 16
HEAD_DIM = 128
INNER = HEADS * HEAD_DIM
CONV_K = 4
LEFT_CHUNKS = 8
LEFT = LEFT_CHUNKS * CHUNK
BAND = LEFT + CHUNK
REL_CLIP = 256
NEG = -1e30
LOG2E = 1.4426950408889634

LANES = 128
SUBLANES = 8
VMEM_LIMIT = 54 * 1024 * 1024

GDN_ROWS = 512
GDN_HEADS = 4
GDN_SPLIT = 1
ATT_ROWS = 512
ATT_HEADS = 4
BAND_PAD = BAND + CHUNK


def _dot(a, b):
    return jnp.dot(a.astype(BF16), b.astype(BF16), preferred_element_type=F32)


def _dot_nt(a, b):
    return lax.dot_general(a.astype(BF16), b.astype(BF16), (((1,), (1,)), ((), ())),
                           preferred_element_type=F32)


def _silu(x):
    return x * jax.nn.sigmoid(x)


def _softplus(x):
    return jnp.maximum(x, 0.0) + jnp.log1p(jnp.exp(-jnp.abs(x)))


def _rms_proj_body(x_ref, nw_ref, *rest, slabs, has_tail):
    w_refs, rest = rest[:slabs], rest[slabs:]
    if has_tail:
        wt_ref, o_ref, ot_ref, hn_ref = rest
    else:
        o_ref, hn_ref = rest

    @pl.when(pl.program_id(1) == 0)
    def _():
        x = x_ref[...]
        ms = jnp.mean(x * x, axis=-1, keepdims=True)
        hn = (x * lax.rsqrt(ms + EPS) * nw_ref[...]).astype(BF16)
        hn_ref[...] = hn
        if has_tail:
            ot_ref[...] = jnp.dot(hn, wt_ref[...], preferred_element_type=F32)

    width = o_ref.shape[1] // slabs
    for p, w_ref in enumerate(w_refs):
        o_ref[:, p * width:(p + 1) * width] = jnp.dot(
            hn_ref[...], w_ref[...], preferred_element_type=F32).astype(o_ref.dtype)


def _rms_proj(x, nw, w, n, w_tail=None, *, slabs=1, tm=1024, tn=2048, out_dtype=BF16):
    m, d = x.shape
    assert m % tm == 0 and n % tn == 0 and n <= w.shape[1] and tn % slabs == 0
    has_tail = w_tail is not None
    col_tiles = n // tn
    in_specs = [pl.BlockSpec((tm, d), lambda i, j: (i, 0)),
                pl.BlockSpec((1, d), lambda i, j: (0, 0))]
    in_specs += [pl.BlockSpec((d, tn // slabs), functools.partial(lambda i, j, p: (0, p * col_tiles + j), p=p))
                 for p in range(slabs)]
    out_specs = [pl.BlockSpec((tm, tn), lambda i, j: (i, j))]
    out_shape = [jax.ShapeDtypeStruct((m, n), out_dtype)]
    args = [x, nw.reshape(1, d)] + [w] * slabs
    if has_tail:
        nt = w_tail.shape[1]
        in_specs.append(pl.BlockSpec((d, nt), lambda i, j: (0, 0)))
        out_specs.append(pl.BlockSpec((tm, nt), lambda i, j: (i, 0)))
        out_shape.append(jax.ShapeDtypeStruct((m, nt), F32))
        args.append(w_tail)
    res = pl.pallas_call(
        functools.partial(_rms_proj_body, slabs=slabs, has_tail=has_tail),
        name="rms_proj_tail" if has_tail else "rms_proj",
        grid=(m // tm, col_tiles),
        in_specs=in_specs, out_specs=out_specs, out_shape=out_shape,
        scratch_shapes=[pltpu.VMEM((tm, d), BF16)],
        compiler_params=pltpu.CompilerParams(
            dimension_semantics=("parallel", "arbitrary"), vmem_limit_bytes=VMEM_LIMIT),
    )(*args)
    return res if has_tail else res[0]


def _out_proj_body(a_ref, w_ref, r_ref, o_ref):
    o_ref[...] = r_ref[...] + jnp.dot(a_ref[...], w_ref[...], preferred_element_type=F32)


def _out_proj(a, w, res, *, tm=512):
    m, k = a.shape
    n = w.shape[1]
    assert m % tm == 0
    return pl.pallas_call(
        _out_proj_body,
        name="out_proj",
        grid=(m // tm,),
        in_specs=[pl.BlockSpec((tm, k), lambda i: (i, 0)),
                  pl.BlockSpec((k, n), lambda i: (0, 0)),
                  pl.BlockSpec((tm, n), lambda i: (i, 0))],
        out_specs=pl.BlockSpec((tm, n), lambda i: (i, 0)),
        out_shape=jax.ShapeDtypeStruct((m, n), F32),
        compiler_params=pltpu.CompilerParams(
            dimension_semantics=("parallel",), vmem_limit_bytes=VMEM_LIMIT),
    )(a, w, res)


def _bdot(a, b):
    return jnp.einsum("cij,cjk->cik", a.astype(BF16), b.astype(BF16), preferred_element_type=F32)


def _bdot_nt(a, b):
    return jnp.einsum("cid,cjd->cij", a.astype(BF16), b.astype(BF16), preferred_element_type=F32)


def _bdot_tn(a, b):
    return jnp.einsum("cjd,cjn->cdn", a.astype(BF16), b.astype(BF16), preferred_element_type=F32)


def _unit_lower_inverse(low, row, col):
    eye = (row == col).astype(F32)
    base = SUBLANES
    ld = jnp.where((row // base) == (col // base), low, 0.0)
    l2 = _bdot(ld, ld)
    l4 = _bdot(l2, l2)
    x = eye - ld
    x = x + _bdot(x, l2)
    x = x + _bdot(x, l4)
    size = base
    low_b = low.astype(BF16)
    while size < CHUNK:
        pair = 2 * size
        off = ((row // pair) == (col // pair)) & ((row % pair) >= size) & ((col % pair) < size)
        c = jnp.where(off, low_b, jnp.zeros_like(low_b))
        x = x - _bdot(x, _bdot(c, x))
        size = pair
    return x


def _gdn_body(alog_ref, dtb_ref, q_ref, k_ref, v_ref, z_ref, cwq_ref, cwk_ref, cwv_ref,
              a_ref, b_ref, onw_ref, o_ref, xq_ref, xk_ref, xv_ref, s_ref):
    rows, width = q_ref.shape
    heads = width // HEAD_DIM
    chunks = rows // CHUNK
    head0 = pl.program_id(1) * heads
    step = pl.program_id(2)

    @pl.when(step == 0)
    def _():
        zeros = jnp.zeros((SUBLANES, width), F32)
        xq_ref[0:SUBLANES, :] = zeros
        xk_ref[0:SUBLANES, :] = zeros
        xv_ref[0:SUBLANES, :] = zeros
        s_ref[...] = jnp.zeros_like(s_ref)

    for x_ref, buf_ref in ((q_ref, xq_ref), (k_ref, xk_ref), (v_ref, xv_ref)):
        buf_ref[SUBLANES:SUBLANES + rows, :] = x_ref[...].astype(F32)

    pos = lax.broadcasted_iota(jnp.int32, (SUBLANES, rows), 1) % CHUNK
    sub = lax.broadcasted_iota(jnp.int32, (SUBLANES, HEAD_DIM), 0)
    row = lax.broadcasted_iota(jnp.int32, (CHUNK, CHUNK), 0)
    col = lax.broadcasted_iota(jnp.int32, (CHUNK, CHUNK), 1)

    def gates(hd):
        a_in = jnp.broadcast_to(a_ref[hd], (SUBLANES, rows))
        b_in = jnp.broadcast_to(b_ref[hd], (SUBLANES, rows))
        neg_rate = -jnp.exp(jnp.full((SUBLANES, rows), alog_ref[head0 + hd], F32))
        gc = neg_rate * _softplus(a_in + dtb_ref[head0 + hd])
        beta = jax.nn.sigmoid(b_in)
        shift = 1
        while shift < CHUNK:
            gc = gc + jnp.where(pos >= shift, pltpu.roll(gc, shift, 1), 0.0)
            shift *= 2
        return (jnp.transpose(beta[0:1, :]).reshape(chunks, CHUNK, 1),
                jnp.transpose(gc[0:1, :]).reshape(chunks, CHUNK, 1),
                jnp.stack([gc[0:1, c * CHUNK:(c + 1) * CHUNK] for c in range(chunks)]))

    def conv_silu(buf_ref, cw_ref, hd, c0, c1):
        lanes = slice(hd * HEAD_DIM, (hd + 1) * HEAD_DIM)
        taps = [jnp.broadcast_to(cw_ref[j:j + 1, lanes], (SUBLANES, HEAD_DIM)) for j in range(CONV_K)]
        tiles = []
        for c in range(c0, c1):
            ext = buf_ref[c * CHUNK:(c + 1) * CHUNK + SUBLANES, lanes]
            ext = ext.reshape(CHUNK // SUBLANES + 1, SUBLANES, HEAD_DIM)
            y = ext[1:] * taps[CONV_K - 1]
            for shift in range(1, CONV_K):
                rot = pltpu.roll(ext, shift, 1)
                y = y + jnp.where(sub < shift, rot[:-1], rot[1:]) * taps[CONV_K - 1 - shift]
            tiles.append(_silu(y).reshape(CHUNK, HEAD_DIM))
        return jnp.stack(tiles)

    head_gates = [gates(hd) for hd in range(heads)]

    def chunk_local(c0, c1):
        parts = {name: [] for name in ("q", "k", "v", "beta", "gc", "gcr")}
        for hd in range(heads):
            beta, gc, gcr = head_gates[hd]
            q = conv_silu(xq_ref, cwq_ref, hd, c0, c1)
            k = conv_silu(xk_ref, cwk_ref, hd, c0, c1)
            parts["q"].append(q * lax.rsqrt(jnp.sum(q * q, axis=-1, keepdims=True) + EPS) * (HEAD_DIM ** -0.5))
            parts["k"].append(k * lax.rsqrt(jnp.sum(k * k, axis=-1, keepdims=True) + EPS))
            parts["v"].append(conv_silu(xv_ref, cwv_ref, hd, c0, c1))
            parts["beta"].append(beta[c0:c1])
            parts["gc"].append(gc[c0:c1])
            parts["gcr"].append(gcr[c0:c1])
        q, k, v, beta, gc, gcr = (jnp.concatenate(parts[n], axis=0) for n in ("q", "k", "v", "beta", "gc", "gcr"))
        gcl = gc[:, CHUNK - 1:CHUNK, :]
        decay = jnp.exp(jnp.where(row >= col, gc - gcr, -jnp.inf))
        egc = jnp.exp(gc)
        kb = k * beta
        low = jnp.where(row > col, _bdot_nt(kb, k) * decay, 0.0)
        inv = _unit_lower_inverse(low, row, col)
        sol = _bdot(inv, jnp.concatenate([v * beta, kb * egc], axis=-1))
        qk = _bdot_nt(q, k) * decay
        k_dec = k * jnp.exp(gcl - gc)
        kt = _bdot_tn(k_dec, sol)
        qs = _bdot(qk, sol)
        lhs = jnp.concatenate([kt[:, :, HEAD_DIM:], q * egc - qs[:, :, HEAD_DIM:]], axis=1)
        return lhs, kt[:, :, :HEAD_DIM], qs[:, :, :HEAD_DIM], jnp.exp(gcl)

    per = chunks // GDN_SPLIT
    local = [chunk_local(g * per, (g + 1) * per) for g in range(GDN_SPLIT)]
    for buf_ref in (xq_ref, xk_ref, xv_ref):
        buf_ref[0:SUBLANES, :] = buf_ref[rows:rows + SUBLANES, :]

    onw = onw_ref[...]
    states = [s_ref[hd] for hd in range(heads)]
    for c in range(chunks):
        r0 = c * CHUNK
        for hd in range(heads):
            lhs, add_s, add_o, carry = local[c // per]
            i = hd * per + c % per
            lanes = slice(hd * HEAD_DIM, (hd + 1) * HEAD_DIM)
            prod = _dot(lhs[i], states[hd])
            states[hd] = states[hd] * carry[i] - prod[:HEAD_DIM] + add_s[i]
            o = prod[HEAD_DIM:] + add_o[i]
            zg = _silu(z_ref[r0:r0 + CHUNK, lanes].astype(F32))
            o = o * lax.rsqrt(jnp.mean(o * o, axis=-1, keepdims=True) + EPS) * onw
            o_ref[r0:r0 + CHUNK, lanes] = (o * zg).astype(o_ref.dtype)
    for hd in range(heads):
        s_ref[hd] = states[hd]


def _gdn(proj, gates_t, conv_w, a_log, dt_bias, out_norm_w, batch, seq):
    rows, heads = GDN_ROWS, GDN_HEADS
    assert seq % rows == 0 and HEADS % heads == 0
    nt, groups, width = seq // rows, HEADS // heads, heads * HEAD_DIM
    blk = lambda off: pl.BlockSpec((rows, width), lambda b, h, t: (b * nt + t, off * groups + h))
    cw = lambda off: pl.BlockSpec((CONV_K, width), lambda b, h, t: (0, off * groups + h))
    gate = lambda off: pl.BlockSpec((None, heads, 1, rows), lambda b, h, t: (b, off * groups + h, 0, t))
    smem = pl.BlockSpec(memory_space=pltpu.SMEM)
    return pl.pallas_call(
        _gdn_body,
        name="gdn",
        grid=(batch, groups, nt),
        in_specs=[smem, smem,
                  blk(0), blk(1), blk(2), blk(3),
                  cw(0), cw(1), cw(2),
                  gate(0), gate(1),
                  pl.BlockSpec((1, HEAD_DIM), lambda b, h, t: (0, 0))],
        out_specs=pl.BlockSpec((rows, width), lambda b, h, t: (b * nt + t, h)),
        out_shape=jax.ShapeDtypeStruct((batch * seq, INNER), BF16),
        scratch_shapes=[pltpu.VMEM((rows + SUBLANES, width), F32)] * 3
                       + [pltpu.VMEM((heads, HEAD_DIM, HEAD_DIM), F32)],
        compiler_params=pltpu.CompilerParams(
            dimension_semantics=("parallel", "parallel", "arbitrary"), vmem_limit_bytes=VMEM_LIMIT),
    )(a_log, dt_bias, proj, proj, proj, proj, conv_w, conv_w, conv_w, gates_t, gates_t,
      out_norm_w.reshape(1, HEAD_DIM))


def _attn_body(x_ref, qnw_ref, knw_ref, bias_ref, o_ref, kbuf_ref, vbuf_ref, s_ref):
    rows, width = o_ref.shape
    step = pl.program_id(2)

    @pl.when(step == 0)
    def _():
        kbuf_ref[0:LEFT, :] = jnp.zeros((LEFT, width), BF16)
        vbuf_ref[0:LEFT, :] = jnp.zeros((LEFT, width), BF16)

    def rms(x, w):
        return x * lax.rsqrt(jnp.mean(x * x, axis=-1, keepdims=True) + EPS) * w

    vbuf_ref[LEFT:LEFT + rows, :] = x_ref[:, 2 * width:3 * width]
    first_key = step * rows - LEFT
    col = lax.broadcasted_iota(jnp.int32, (1, BAND_PAD), 1)
    for hd in range(width // HEAD_DIM):
        lanes = slice(hd * HEAD_DIM, (hd + 1) * HEAD_DIM)
        q_lanes, k_lanes, z_lanes = (slice(part * width + lanes.start, part * width + lanes.stop)
                                     for part in (0, 1, 3))
        q = rms(x_ref[:, q_lanes].astype(F32), qnw_ref[...]) * (HEAD_DIM ** -0.5 * LOG2E)
        kbuf_ref[LEFT:LEFT + rows, lanes] = rms(x_ref[:, k_lanes].astype(F32), knw_ref[...]).astype(BF16)
        s_ref[hd] = _dot_nt(q, kbuf_ref[:, lanes])
        for c in range(rows // CHUNK):
            r0 = c * CHUNK
            lo = (c // 2) * LANES
            before_start = jnp.where(first_key + lo + col >= 0, 0.0, NEG)
            s = s_ref[hd, r0:r0 + CHUNK, lo:lo + BAND_PAD] + bias_ref[hd, c % 2] + before_start
            p = jnp.exp2(s - jnp.max(s, axis=-1, keepdims=True))
            denom = jnp.sum(p, axis=-1, keepdims=True)
            o = _dot(p, vbuf_ref[lo:lo + BAND_PAD, lanes]) / denom
            o = o * _silu(x_ref[r0:r0 + CHUNK, z_lanes].astype(F32))
            o_ref[r0:r0 + CHUNK, lanes] = o.astype(o_ref.dtype)

    kbuf_ref[0:LEFT, :] = kbuf_ref[rows:rows + LEFT, :]
    vbuf_ref[0:LEFT, :] = vbuf_ref[rows:rows + LEFT, :]


def _attn(proj, q_norm_w, k_norm_w, bias_tab, batch, seq):
    rows, heads = ATT_ROWS, ATT_HEADS
    assert seq % rows == 0 and rows >= LEFT and rows % LANES == 0 and HEADS % heads == 0
    nt, groups, width = seq // rows, HEADS // heads, heads * HEAD_DIM
    vec = pl.BlockSpec((1, HEAD_DIM), lambda b, h, t: (0, 0))
    return pl.pallas_call(
        _attn_body,
        name="band_attn",
        grid=(batch, groups, nt),
        in_specs=[pl.BlockSpec((rows, 4 * width), lambda b, h, t: (b * nt + t, h)), vec, vec,
                  pl.BlockSpec((heads, 2, CHUNK, BAND_PAD), lambda b, h, t: (h, 0, 0, 0))],
        out_specs=pl.BlockSpec((rows, width), lambda b, h, t: (b * nt + t, h)),
        out_shape=jax.ShapeDtypeStruct((batch * seq, INNER), BF16),
        scratch_shapes=[pltpu.VMEM((LEFT + rows, width), BF16),
                        pltpu.VMEM((LEFT + rows, width), BF16),
                        pltpu.VMEM((heads, rows, LEFT + rows), F32)],
        compiler_params=pltpu.CompilerParams(
            dimension_semantics=("parallel", "parallel", "arbitrary"), vmem_limit_bytes=VMEM_LIMIT),
    )(proj, q_norm_w.reshape(1, HEAD_DIM), k_norm_w.reshape(1, HEAD_DIM), bias_tab)


def _bias_table(rel_bias):
    t = jnp.arange(BAND + CHUNK - 1)
    idx = jnp.clip(LEFT + CHUNK - 1 - t, -REL_CLIP, REL_CLIP) + REL_CLIP
    diag = rel_bias.astype(F32)[:, idx] * LOG2E
    band = jnp.stack([diag[:, CHUNK - 1 - r:CHUNK - 1 - r + BAND] for r in range(CHUNK)], axis=1)
    pad = jnp.full((HEADS, CHUNK, CHUNK), NEG, F32)
    even = jnp.concatenate([band, pad], axis=-1)
    odd = jnp.concatenate([pad, band], axis=-1)
    return jnp.stack([even, odd], axis=1)


def kernel(x, norm_w, a_w_in, a_conv_w, a_a_log, a_dt_bias, a_out_norm_w, a_w_out,
           b_w_in, b_q_norm_w, b_k_norm_w, b_rel_bias, b_w_out):
    batch, seq, d = x.shape
    h0 = x.reshape(batch * seq, d)

    w_in = a_w_in[0]
    w_tail = jnp.pad(w_in[:, 4 * INNER:], ((0, 0), (0, LANES - 2 * HEADS))).astype(BF16)
    proj, tail = _rms_proj(h0, norm_w[0], w_in.astype(BF16), 4 * INNER, w_tail)
    gates_t = tail[:, :2 * HEADS].reshape(batch, seq, 2 * HEADS).transpose(0, 2, 1)
    gates_t = gates_t.reshape(batch, 2 * HEADS, 1, seq)
    mixed = _gdn(proj, gates_t, a_conv_w[0], a_a_log[0], a_dt_bias[0], a_out_norm_w[0], batch, seq)
    h1 = _out_proj(mixed, a_w_out[0].astype(BF16), h0)

    proj = _rms_proj(h1, norm_w[1], b_w_in[0].astype(BF16), 4 * INNER, slabs=4, tn=4 * ATT_HEADS * HEAD_DIM)
    mixed = _attn(proj, b_q_norm_w[0], b_k_norm_w[0], _bias_table(b_rel_bias[0]), batch, seq)
    h2 = _out_proj(mixed, b_w_out[0].astype(BF16), h1)
    return h2.reshape(batch, seq, d)
```

```python
import functools

import jax
import jax.numpy as jnp
from jax import lax
from jax.experimental import pallas as pl
from jax.experimental.pallas import tpu as pltpu

F32 = jnp.float32
BF16 = jnp.bfloat16

EPS = 1e-6
CHUNK = 64
HEADS = 16
HEAD_DIM = 128
INNER = HEADS * HEAD_DIM
CONV_K = 4
LEFT_CHUNKS = 8
LEFT = LEFT_CHUNKS * CHUNK
BAND = LEFT + CHUNK
REL_CLIP = 256
NEG = -1e30
LOG2E = 1.4426950408889634

LANES = 128
SUBLANES = 8
VMEM_LIMIT = 54 * 1024 * 1024

GDN_ROWS = 512
GDN_HEADS = 4
GDN_SPLIT = 1
ATT_ROWS = 512
ATT_HEADS = 4
BAND_PAD = BAND + CHUNK


def _dot(a, b):
    return jnp.dot(a.astype(BF16), b.astype(BF16), preferred_element_type=F32)


def _dot_nt(a, b):
    return lax.dot_general(a.astype(BF16), b.astype(BF16), (((1,), (1,)), ((), ())),
                           preferred_element_type=F32)


def _silu(x):
    return x * jax.nn.sigmoid(x)


def _softplus(x):
    return jnp.maximum(x, 0.0) + jnp.log1p(jnp.exp(-jnp.abs(x)))


def _rms_proj_body(x_ref, nw_ref, *rest, slabs, has_tail):
    w_refs, rest = rest[:slabs], rest[slabs:]
    if has_tail:
        wt_ref, o_ref, ot_ref, hn_ref = rest
    else:
        o_ref, hn_ref = rest

    @pl.when(pl.program_id(1) == 0)
    def _():
        x = x_ref[...]
        ms = jnp.mean(x * x, axis=-1, keepdims=True)
        hn = (x * lax.rsqrt(ms + EPS) * nw_ref[...]).astype(BF16)
        hn_ref[...] = hn
        if has_tail:
            ot_ref[...] = jnp.dot(hn, wt_ref[...], preferred_element_type=F32)

    width = o_ref.shape[1] // slabs
    for p, w_ref in enumerate(w_refs):
        o_ref[:, p * width:(p + 1) * width] = jnp.dot(
            hn_ref[...], w_ref[...], preferred_element_type=F32).astype(o_ref.dtype)


def _rms_proj(x, nw, w, n, w_tail=None, *, slabs=1, tm=1024, tn=2048):
    m, d = x.shape
    assert m % tm == 0 and n % tn == 0 and n <= w.shape[1] and tn % slabs == 0
    has_tail = w_tail is not None
    col_tiles = n // tn
    in_specs = [pl.BlockSpec((tm, d), lambda i, j: (i, 0)),
                pl.BlockSpec((1, d), lambda i, j: (0, 0))]
    in_specs += [pl.BlockSpec((d, tn // slabs), functools.partial(lambda i, j, p: (0, p * col_tiles + j), p=p))
                 for p in range(slabs)]
    out_specs = [pl.BlockSpec((tm, tn), lambda i, j: (i, j))]
    out_shape = [jax.ShapeDtypeStruct((m, n), BF16)]
    args = [x, nw.reshape(1, d)] + [w] * slabs
    if has_tail:
        nt = w_tail.shape[1]
        in_specs.append(pl.BlockSpec((d, nt), lambda i, j: (0, 0)))
        out_specs.append(pl.BlockSpec((tm, nt), lambda i, j: (i, 0)))
        out_shape.append(jax.ShapeDtypeStruct((m, nt), F32))
        args.append(w_tail)
    res = pl.pallas_call(
        functools.partial(_rms_proj_body, slabs=slabs, has_tail=has_tail),
        name="rms_proj_tail" if has_tail else "rms_proj",
        grid=(m // tm, col_tiles),
        in_specs=in_specs, out_specs=out_specs, out_shape=out_shape,
        scratch_shapes=[pltpu.VMEM((tm, d), BF16)],
        compiler_params=pltpu.CompilerParams(
            dimension_semantics=("parallel", "arbitrary"), vmem_limit_bytes=VMEM_LIMIT),
    )(*args)
    return res if has_tail else res[0]


def _out_proj_body(a_ref, w_ref, r_ref, o_ref):
    o_ref[...] = r_ref[...] + jnp.dot(a_ref[...], w_ref[...], preferred_element_type=F32)


def _out_proj(a, w, res, *, tm=512):
    m, k = a.shape
    n = w.shape[1]
    assert m % tm == 0
    return pl.pallas_call(
        _out_proj_body,
        name="out_proj",
        grid=(m // tm,),
        in_specs=[pl.BlockSpec((tm, k), lambda i: (i, 0)),
                  pl.BlockSpec((k, n), lambda i: (0, 0)),
                  pl.BlockSpec((tm, n), lambda i: (i, 0))],
        out_specs=pl.BlockSpec((tm, n), lambda i: (i, 0)),
        out_shape=jax.ShapeDtypeStruct((m, n), F32),
        compiler_params=pltpu.CompilerParams(
            dimension_semantics=("parallel",), vmem_limit_bytes=VMEM_LIMIT),
    )(a, w, res)


def _bdot(a, b):
    return jnp.einsum("cij,cjk->cik", a.astype(BF16), b.astype(BF16), preferred_element_type=F32)


def _bdot_nt(a, b):
    return jnp.einsum("cid,cjd->cij", a.astype(BF16), b.astype(BF16), preferred_element_type=F32)


def _bdot_tn(a, b):
    return jnp.einsum("cjd,cjn->cdn", a.astype(BF16), b.astype(BF16), preferred_element_type=F32)


def _unit_lower_inverse(low, row, col):
    eye = (row == col).astype(F32)
    base = SUBLANES
    ld = jnp.where((row // base) == (col // base), low, 0.0)
    l2 = _bdot(ld, ld)
    l4 = _bdot(l2, l2)
    x = eye - ld
    x = x + _bdot(x, l2)
    x = x + _bdot(x, l4)
    size = base
    low_b = low.astype(BF16)
    while size < CHUNK:
        pair = 2 * size
        off = ((row // pair) == (col // pair)) & ((row % pair) >= size) & ((col % pair) < size)
        c = jnp.where(off, low_b, jnp.zeros_like(low_b))
        x = x - _bdot(x, _bdot(c, x))
        size = pair
    return x


def _gdn_body(alog_ref, dtb_ref, q_ref, k_ref, v_ref, z_ref, cwq_ref, cwk_ref, cwv_ref,
              a_ref, b_ref, onw_ref, o_ref, xq_ref, xk_ref, xv_ref, s_ref):
    rows, width = q_ref.shape
    heads = width // HEAD_DIM
    chunks = rows // CHUNK
    head0 = pl.program_id(1) * heads
    step = pl.program_id(2)

    @pl.when(step == 0)
    def _():
        zeros = jnp.zeros((SUBLANES, width), F32)
        xq_ref[0:SUBLANES, :] = zeros
        xk_ref[0:SUBLANES, :] = zeros
        xv_ref[0:SUBLANES, :] = zeros
        s_ref[...] = jnp.zeros_like(s_ref)

    for x_ref, buf_ref in ((q_ref, xq_ref), (k_ref, xk_ref), (v_ref, xv_ref)):
        buf_ref[SUBLANES:SUBLANES + rows, :] = x_ref[...].astype(F32)

    pos = lax.broadcasted_iota(jnp.int32, (SUBLANES, rows), 1) % CHUNK
    sub = lax.broadcasted_iota(jnp.int32, (SUBLANES, HEAD_DIM), 0)
    row = lax.broadcasted_iota(jnp.int32, (CHUNK, CHUNK), 0)
    col = lax.broadcasted_iota(jnp.int32, (CHUNK, CHUNK), 1)

    def gates(hd):
        a_in = jnp.broadcast_to(a_ref[hd], (SUBLANES, rows))
        b_in = jnp.broadcast_to(b_ref[hd], (SUBLANES, rows))
        neg_rate = -jnp.exp(jnp.full((SUBLANES, rows), alog_ref[head0 + hd], F32))
        gc = neg_rate * _softplus(a_in + dtb_ref[head0 + hd])
        beta = jax.nn.sigmoid(b_in)
        shift = 1
        while shift < CHUNK:
            gc = gc + jnp.where(pos >= shift, pltpu.roll(gc, shift, 1), 0.0)
            shift *= 2
        return (jnp.transpose(beta[0:1, :]).reshape(chunks, CHUNK, 1),
                jnp.transpose(gc[0:1, :]).reshape(chunks, CHUNK, 1),
                jnp.stack([gc[0:1, c * CHUNK:(c + 1) * CHUNK] for c in range(chunks)]))

    def conv_silu(buf_ref, cw_ref, hd, c0, c1):
        lanes = slice(hd * HEAD_DIM, (hd + 1) * HEAD_DIM)
        taps = [jnp.broadcast_to(cw_ref[j:j + 1, lanes], (SUBLANES, HEAD_DIM)) for j in range(CONV_K)]
        tiles = []
        for c in range(c0, c1):
            ext = buf_ref[c * CHUNK:(c + 1) * CHUNK + SUBLANES, lanes]
            ext = ext.reshape(CHUNK // SUBLANES + 1, SUBLANES, HEAD_DIM)
            y = ext[1:] * taps[CONV_K - 1]
            for shift in range(1, CONV_K):
                rot = pltpu.roll(ext, shift, 1)
                y = y + jnp.where(sub < shift, rot[:-1], rot[1:]) * taps[CONV_K - 1 - shift]
            tiles.append(_silu(y).reshape(CHUNK, HEAD_DIM))
        return jnp.stack(tiles)

    head_gates = [gates(hd) for hd in range(heads)]

    def chunk_local(c0, c1):
        parts = {name: [] for name in ("q", "k", "v", "beta", "gc", "gcr")}
        for hd in range(heads):
            beta, gc, gcr = head_gates[hd]
            q = conv_silu(xq_ref, cwq_ref, hd, c0, c1)
            k = conv_silu(xk_ref, cwk_ref, hd, c0, c1)
            parts["q"].append(q * lax.rsqrt(jnp.sum(q * q, axis=-1, keepdims=True) + EPS) * (HEAD_DIM ** -0.5))
            parts["k"].append(k * lax.rsqrt(jnp.sum(k * k, axis=-1, keepdims=True) + EPS))
            parts["v"].append(conv_silu(xv_ref, cwv_ref, hd, c0, c1))
            parts["beta"].append(beta[c0:c1])
            parts["gc"].append(gc[c0:c1])
            parts["gcr"].append(gcr[c0:c1])
        q, k, v, beta, gc, gcr = (jnp.concatenate(parts[n], axis=0) for n in ("q", "k", "v", "beta", "gc", "gcr"))
        gcl = gc[:, CHUNK - 1:CHUNK, :]
        decay = jnp.exp(jnp.where(row >= col, gc - gcr, -jnp.inf))
        egc = jnp.exp(gc)
        kb = k * beta
        low = jnp.where(row > col, _bdot_nt(kb, k) * decay, 0.0)
        inv = _unit_lower_inverse(low, row, col)
        sol = _bdot(inv, jnp.concatenate([v * beta, kb * egc], axis=-1))
        qk = _bdot_nt(q, k) * decay
        k_dec = k * jnp.exp(gcl - gc)
        kt = _bdot_tn(k_dec, sol)
        qs = _bdot(qk, sol)
        lhs = jnp.concatenate([kt[:, :, HEAD_DIM:], q * egc - qs[:, :, HEAD_DIM:]], axis=1)
        return lhs, kt[:, :, :HEAD_DIM], qs[:, :, :HEAD_DIM], jnp.exp(gcl)

    per = chunks // GDN_SPLIT
    local = [chunk_local(g * per, (g + 1) * per) for g in range(GDN_SPLIT)]
    for buf_ref in (xq_ref, xk_ref, xv_ref):
        buf_ref[0:SUBLANES, :] = buf_ref[rows:rows + SUBLANES, :]

    onw = onw_ref[...]
    states = [s_ref[hd] for hd in range(heads)]
    for c in range(chunks):
        r0 = c * CHUNK
        for hd in range(heads):
            lhs, add_s, add_o, carry = local[c // per]
            i = hd * per + c % per
            lanes = slice(hd * HEAD_DIM, (hd + 1) * HEAD_DIM)
            prod = _dot(lhs[i], states[hd])
            states[hd] = states[hd] * carry[i] - prod[:HEAD_DIM] + add_s[i]
            o = prod[HEAD_DIM:] + add_o[i]
            zg = _silu(z_ref[r0:r0 + CHUNK, lanes].astype(F32))
            o = o * lax.rsqrt(jnp.mean(o * o, axis=-1, keepdims=True) + EPS) * onw
            o_ref[r0:r0 + CHUNK, lanes] = (o * zg).astype(o_ref.dtype)
    for hd in range(heads):
        s_ref[hd] = states[hd]


def _gdn(proj, gates_t, conv_w, a_log, dt_bias, out_norm_w, batch, seq):
    rows, heads = GDN_ROWS, GDN_HEADS
    assert seq % rows == 0 and HEADS % heads == 0
    nt, groups, width = seq // rows, HEADS // heads, heads * HEAD_DIM
    blk = lambda off: pl.BlockSpec((rows, width), lambda b, h, t: (b * nt + t, off * groups + h))
    cw = lambda off: pl.BlockSpec((CONV_K, width), lambda b, h, t: (0, off * groups + h))
    gate = lambda off: pl.BlockSpec((None, heads, 1, rows), lambda b, h, t: (b, off * groups + h, 0, t))
    smem = pl.BlockSpec(memory_space=pltpu.SMEM)
    return pl.pallas_call(
        _gdn_body,
        name="gdn",
        grid=(batch, groups, nt),
        in_specs=[smem, smem,
                  blk(0), blk(1), blk(2), blk(3),
                  cw(0), cw(1), cw(2),
                  gate(0), gate(1),
                  pl.BlockSpec((1, HEAD_DIM), lambda b, h, t: (0, 0))],
        out_specs=pl.BlockSpec((rows, width), lambda b, h, t: (b * nt + t, h)),
        out_shape=jax.ShapeDtypeStruct((batch * seq, INNER), BF16),
        scratch_shapes=[pltpu.VMEM((rows + SUBLANES, width), F32)] * 3
                       + [pltpu.VMEM((heads, HEAD_DIM, HEAD_DIM), F32)],
        compiler_params=pltpu.CompilerParams(
            dimension_semantics=("parallel", "parallel", "arbitrary"), vmem_limit_bytes=VMEM_LIMIT),
    )(a_log, dt_bias, proj, proj, proj, proj, conv_w, conv_w, conv_w, gates_t, gates_t,
      out_norm_w.reshape(1, HEAD_DIM))


def _attn_body(x_ref, qnw_ref, knw_ref, bias_ref, o_ref, kbuf_ref, vbuf_ref, s_ref):
    rows, width = o_ref.shape
    step = pl.program_id(2)

    @pl.when(step == 0)
    def _():
        kbuf_ref[0:LEFT, :] = jnp.zeros((LEFT, width), BF16)
        vbuf_ref[0:LEFT, :] = jnp.zeros((LEFT, width), BF16)

    def rms(x, w):
        return x * lax.rsqrt(jnp.mean(x * x, axis=-1, keepdims=True) + EPS) * w

    vbuf_ref[LEFT:LEFT + rows, :] = x_ref[:, 2 * width:3 * width]
    first_key = step * rows - LEFT
    col = lax.broadcasted_iota(jnp.int32, (1, BAND_PAD), 1)
    for hd in range(width // HEAD_DIM):
        lanes = slice(hd * HEAD_DIM, (hd + 1) * HEAD_DIM)
        q_lanes, k_lanes, z_lanes = (slice(part * width + lanes.start, part * width + lanes.stop)
                                     for part in (0, 1, 3))
        q = rms(x_ref[:, q_lanes].astype(F32), qnw_ref[...] * (HEAD_DIM ** -0.5 * LOG2E))
        kbuf_ref[LEFT:LEFT + rows, lanes] = rms(x_ref[:, k_lanes].astype(F32), knw_ref[...]).astype(BF16)
        for lo in range(0, rows, LANES):
            s_ref[hd, lo:lo + LANES, :] = _dot_nt(q[lo:lo + LANES], kbuf_ref[lo:lo + BAND_PAD, lanes])
        for c in range(rows // CHUNK):
            r0 = c * CHUNK
            lo = (c // 2) * LANES
            before_start = jnp.where(first_key + lo + col >= 0, 0.0, NEG)
            s = s_ref[hd, r0:r0 + CHUNK, :] + bias_ref[hd, c % 2] + before_start
            p = jnp.exp2(s - jnp.max(s, axis=-1, keepdims=True))
            denom = jnp.sum(p, axis=-1, keepdims=True)
            o = _dot(p, vbuf_ref[lo:lo + BAND_PAD, lanes]) / denom
            o = o * _silu(x_ref[r0:r0 + CHUNK, z_lanes].astype(F32))
            o_ref[r0:r0 + CHUNK, lanes] = o.astype(o_ref.dtype)

    kbuf_ref[0:LEFT, :] = kbuf_ref[rows:rows + LEFT, :]
    vbuf_ref[0:LEFT, :] = vbuf_ref[rows:rows + LEFT, :]


def _attn(proj, q_norm_w, k_norm_w, bias_tab, batch, seq):
    rows, heads = ATT_ROWS, ATT_HEADS
    assert seq % rows == 0 and rows >= LEFT and rows % LANES == 0 and HEADS % heads == 0
    nt, groups, width = seq // rows, HEADS // heads, heads * HEAD_DIM
    vec = pl.BlockSpec((1, HEAD_DIM), lambda b, h, t: (0, 0))
    return pl.pallas_call(
        _attn_body,
        name="band_attn",
        grid=(batch, groups, nt),
        in_specs=[pl.BlockSpec((rows, 4 * width), lambda b, h, t: (b * nt + t, h)), vec, vec,
                  pl.BlockSpec((heads, 2, CHUNK, BAND_PAD), lambda b, h, t: (h, 0, 0, 0))],
        out_specs=pl.BlockSpec((rows, width), lambda b, h, t: (b * nt + t, h)),
        out_shape=jax.ShapeDtypeStruct((batch * seq, INNER), BF16),
        scratch_shapes=[pltpu.VMEM((LEFT + rows, width), BF16),
                        pltpu.VMEM((LEFT + rows, width), BF16),
                        pltpu.VMEM((heads, rows, BAND_PAD), F32)],
        compiler_params=pltpu.CompilerParams(
            dimension_semantics=("parallel", "parallel", "arbitrary"), vmem_limit_bytes=VMEM_LIMIT),
    )(proj, q_norm_w.reshape(1, HEAD_DIM), k_norm_w.reshape(1, HEAD_DIM), bias_tab)


def _bias_table(rel_bias):
    t = jnp.arange(BAND + CHUNK - 1)
    idx = jnp.clip(LEFT + CHUNK - 1 - t, -REL_CLIP, REL_CLIP) + REL_CLIP
    diag = rel_bias.astype(F32)[:, idx] * LOG2E
    band = jnp.stack([diag[:, CHUNK - 1 - r:CHUNK - 1 - r + BAND] for r in range(CHUNK)], axis=1)
    pad = jnp.full((HEADS, CHUNK, CHUNK), NEG, F32)
    even = jnp.concatenate([band, pad], axis=-1)
    odd = jnp.concatenate([pad, band], axis=-1)
    return jnp.stack([even, odd], axis=1)


def kernel(x, norm_w, a_w_in, a_conv_w, a_a_log, a_dt_bias, a_out_norm_w, a_w_out,
           b_w_in, b_q_norm_w, b_k_norm_w, b_rel_bias, b_w_out):
    batch, seq, d = x.shape
    h0 = x.reshape(batch * seq, d)

    w_in = a_w_in[0]
    w_tail = jnp.pad(w_in[:, 4 * INNER:], ((0, 0), (0, LANES - 2 * HEADS))).astype(BF16)
    proj, tail = _rms_proj(h0, norm_w[0], w_in.astype(BF16), 4 * INNER, w_tail)
    gates_t = tail[:, :2 * HEADS].reshape(batch, seq, 2 * HEADS).transpose(0, 2, 1)
    gates_t = gates_t.reshape(batch, 2 * HEADS, 1, seq)
    mixed = _gdn(proj, gates_t, a_conv_w[0], a_a_log[0], a_dt_bias[0], a_out_norm_w[0], batch, seq)
    h1 = _out_proj(mixed, a_w_out[0].astype(BF16), h0)

    proj = _rms_proj(h1, norm_w[1], b_w_in[0].astype(BF16), 4 * INNER, slabs=4, tn=4 * ATT_HEADS * HEAD_DIM)
    mixed = _attn(proj, b_q_norm_w[0], b_k_norm_w[0], _bias_table(b_rel_bias[0]), batch, seq)
    h2 = _out_proj(mixed, b_w_out[0].astype(BF16), h1)
    return h2.reshape(batch, seq, d)
```

```python
import functools

import jax
import jax.numpy as jnp
from jax import lax
from jax.experimental import pallas as pl
from jax.experimental.pallas import tpu as pltpu

F32 = jnp.float32
BF16 = jnp.bfloat16

EPS = 1e-6
CHUNK = 64
HEADS = 16
HEAD_DIM = 128
INNER = HEADS * HEAD_DIM
CONV_K = 4
LEFT_CHUNKS = 8
LEFT = LEFT_CHUNKS * CHUNK
BAND = LEFT + CHUNK
REL_CLIP = 256
NEG = -1e30
LOG2E = 1.4426950408889634

LANES = 128
SUBLANES = 8
VMEM_LIMIT = 54 * 1024 * 1024

GDN_ROWS = 512
GDN_HEADS = 4
GDN_SPLIT = 1
ATT_ROWS = 1024
ATT_HEADS = 4
BAND_PAD = BAND + CHUNK


def _dot(a, b):
    return jnp.dot(a.astype(BF16), b.astype(BF16), preferred_element_type=F32)


def _dot_nt(a, b):
    return lax.dot_general(a.astype(BF16), b.astype(BF16), (((1,), (1,)), ((), ())),
                           preferred_element_type=F32)


def _silu(x):
    return x * jax.nn.sigmoid(x)


def _softplus(x):
    return jnp.maximum(x, 0.0) + jnp.log1p(jnp.exp(-jnp.abs(x)))


def _rms_proj_body(x_ref, nw_ref, *rest, slabs, has_tail):
    w_refs, rest = rest[:slabs], rest[slabs:]
    if has_tail:
        wt_ref, o_ref, ot_ref, hn_ref = rest
    else:
        o_ref, hn_ref = rest

    @pl.when(pl.program_id(1) == 0)
    def _():
        x = x_ref[...]
        ms = jnp.mean(x * x, axis=-1, keepdims=True)
        hn = (x * lax.rsqrt(ms + EPS) * nw_ref[...]).astype(BF16)
        hn_ref[...] = hn
        if has_tail:
            ot_ref[...] = jnp.dot(hn, wt_ref[...], preferred_element_type=F32)

    width = o_ref.shape[1] // slabs
    for p, w_ref in enumerate(w_refs):
        o_ref[:, p * width:(p + 1) * width] = jnp.dot(
            hn_ref[...], w_ref[...], preferred_element_type=F32).astype(o_ref.dtype)


def _rms_proj(x, nw, w, n, w_tail=None, *, slabs=1, tm=1024, tn=2048):
    m, d = x.shape
    assert m % tm == 0 and n % tn == 0 and n <= w.shape[1] and tn % slabs == 0
    has_tail = w_tail is not None
    col_tiles = n // tn
    in_specs = [pl.BlockSpec((tm, d), lambda i, j: (i, 0)),
                pl.BlockSpec((1, d), lambda i, j: (0, 0))]
    in_specs += [pl.BlockSpec((d, tn // slabs), functools.partial(lambda i, j, p: (0, p * col_tiles + j), p=p))
                 for p in range(slabs)]
    out_specs = [pl.BlockSpec((tm, tn), lambda i, j: (i, j))]
    out_shape = [jax.ShapeDtypeStruct((m, n), BF16)]
    args = [x, nw.reshape(1, d)] + [w] * slabs
    if has_tail:
        nt = w_tail.shape[1]
        in_specs.append(pl.BlockSpec((d, nt), lambda i, j: (0, 0)))
        out_specs.append(pl.BlockSpec((tm, nt), lambda i, j: (i, 0)))
        out_shape.append(jax.ShapeDtypeStruct((m, nt), F32))
        args.append(w_tail)
    res = pl.pallas_call(
        functools.partial(_rms_proj_body, slabs=slabs, has_tail=has_tail),
        name="rms_proj_tail" if has_tail else "rms_proj",
        grid=(m // tm, col_tiles),
        in_specs=in_specs, out_specs=out_specs, out_shape=out_shape,
        scratch_shapes=[pltpu.VMEM((tm, d), BF16)],
        compiler_params=pltpu.CompilerParams(
            dimension_semantics=("parallel", "arbitrary"), vmem_limit_bytes=VMEM_LIMIT),
    )(*args)
    return res if has_tail else res[0]


def _out_proj_body(a_ref, w_ref, r_ref, o_ref):
    o_ref[...] = r_ref[...] + jnp.dot(a_ref[...], w_ref[...], preferred_element_type=F32)


def _out_proj(a, w, res, *, tm=512):
    m, k = a.shape
    n = w.shape[1]
    assert m % tm == 0
    return pl.pallas_call(
        _out_proj_body,
        name="out_proj",
        grid=(m // tm,),
        in_specs=[pl.BlockSpec((tm, k), lambda i: (i, 0)),
                  pl.BlockSpec((k, n), lambda i: (0, 0)),
                  pl.BlockSpec((tm, n), lambda i: (i, 0))],
        out_specs=pl.BlockSpec((tm, n), lambda i: (i, 0)),
        out_shape=jax.ShapeDtypeStruct((m, n), F32),
        compiler_params=pltpu.CompilerParams(
            dimension_semantics=("parallel",), vmem_limit_bytes=VMEM_LIMIT),
    )(a, w, res)


def _bdot(a, b):
    return jnp.einsum("cij,cjk->cik", a.astype(BF16), b.astype(BF16), preferred_element_type=F32)


def _bdot_nt(a, b):
    return jnp.einsum("cid,cjd->cij", a.astype(BF16), b.astype(BF16), preferred_element_type=F32)


def _bdot_tn(a, b):
    return jnp.einsum("cjd,cjn->cdn", a.astype(BF16), b.astype(BF16), preferred_element_type=F32)


def _unit_lower_inverse(low, row, col):
    eye = (row == col).astype(F32)
    base = SUBLANES
    ld = jnp.where((row // base) == (col // base), low, 0.0)
    l2 = _bdot(ld, ld)
    l4 = _bdot(l2, l2)
    x = eye - ld
    x = x + _bdot(x, l2)
    x = x + _bdot(x, l4)
    size = base
    low_b = low.astype(BF16)
    while size < CHUNK:
        pair = 2 * size
        off = ((row // pair) == (col // pair)) & ((row % pair) >= size) & ((col % pair) < size)
        c = jnp.where(off, low_b, jnp.zeros_like(low_b))
        x = x - _bdot(x, _bdot(c, x))
        size = pair
    return x


def _gdn_body(alog_ref, dtb_ref, x_ref, cwq_ref, cwk_ref, cwv_ref,
              a_ref, b_ref, onw_ref, o_ref, xq_ref, xk_ref, xv_ref, s_ref):
    rows, width = o_ref.shape
    heads = width // HEAD_DIM
    chunks = rows // CHUNK
    head0 = pl.program_id(1) * heads
    step = pl.program_id(2)

    @pl.when(step == 0)
    def _():
        zeros = jnp.zeros((SUBLANES, width), F32)
        xq_ref[0:SUBLANES, :] = zeros
        xk_ref[0:SUBLANES, :] = zeros
        xv_ref[0:SUBLANES, :] = zeros
        s_ref[...] = jnp.zeros_like(s_ref)

    for part, buf_ref in enumerate((xq_ref, xk_ref, xv_ref)):
        buf_ref[SUBLANES:SUBLANES + rows, :] = x_ref[:, part * width:(part + 1) * width].astype(F32)

    pos = lax.broadcasted_iota(jnp.int32, (SUBLANES, rows), 1) % CHUNK
    sub = lax.broadcasted_iota(jnp.int32, (SUBLANES, HEAD_DIM), 0)
    row = lax.broadcasted_iota(jnp.int32, (CHUNK, CHUNK), 0)
    col = lax.broadcasted_iota(jnp.int32, (CHUNK, CHUNK), 1)

    def gates(hd):
        a_in = jnp.broadcast_to(a_ref[hd], (SUBLANES, rows))
        b_in = jnp.broadcast_to(b_ref[hd], (SUBLANES, rows))
        neg_rate = -jnp.exp(jnp.full((SUBLANES, rows), alog_ref[head0 + hd], F32))
        gc = neg_rate * _softplus(a_in + dtb_ref[head0 + hd])
        beta = jax.nn.sigmoid(b_in)
        shift = 1
        while shift < CHUNK:
            gc = gc + jnp.where(pos >= shift, pltpu.roll(gc, shift, 1), 0.0)
            shift *= 2
        return (jnp.transpose(beta[0:1, :]).reshape(chunks, CHUNK, 1),
                jnp.transpose(gc[0:1, :]).reshape(chunks, CHUNK, 1),
                jnp.stack([gc[0:1, c * CHUNK:(c + 1) * CHUNK] for c in range(chunks)]))

    def conv_silu(buf_ref, cw_ref, hd, c0, c1):
        lanes = slice(hd * HEAD_DIM, (hd + 1) * HEAD_DIM)
        taps = [jnp.broadcast_to(cw_ref[j:j + 1, lanes], (SUBLANES, HEAD_DIM)) for j in range(CONV_K)]
        tiles = []
        for c in range(c0, c1):
            ext = buf_ref[c * CHUNK:(c + 1) * CHUNK + SUBLANES, lanes]
            ext = ext.reshape(CHUNK // SUBLANES + 1, SUBLANES, HEAD_DIM)
            y = ext[1:] * taps[CONV_K - 1]
            for shift in range(1, CONV_K):
                rot = pltpu.roll(ext, shift, 1)
                y = y + jnp.where(sub < shift, rot[:-1], rot[1:]) * taps[CONV_K - 1 - shift]
            tiles.append(_silu(y).reshape(CHUNK, HEAD_DIM))
        return jnp.stack(tiles)

    head_gates = [gates(hd) for hd in range(heads)]

    def chunk_local(c0, c1):
        parts = {name: [] for name in ("q", "k", "v", "beta", "gc", "gcr")}
        for hd in range(heads):
            beta, gc, gcr = head_gates[hd]
            q = conv_silu(xq_ref, cwq_ref, hd, c0, c1)
            k = conv_silu(xk_ref, cwk_ref, hd, c0, c1)
            parts["q"].append(q * lax.rsqrt(jnp.sum(q * q, axis=-1, keepdims=True) + EPS) * (HEAD_DIM ** -0.5))
            parts["k"].append(k * lax.rsqrt(jnp.sum(k * k, axis=-1, keepdims=True) + EPS))
            parts["v"].append(conv_silu(xv_ref, cwv_ref, hd, c0, c1))
            parts["beta"].append(beta[c0:c1])
            parts["gc"].append(gc[c0:c1])
            parts["gcr"].append(gcr[c0:c1])
        q, k, v, beta, gc, gcr = (jnp.concatenate(parts[n], axis=0) for n in ("q", "k", "v", "beta", "gc", "gcr"))
        gcl = gc[:, CHUNK - 1:CHUNK, :]
        decay = jnp.exp(jnp.where(row >= col, gc - gcr, -jnp.inf))
        egc = jnp.exp(gc)
        kb = k * beta
        low = jnp.where(row > col, _bdot_nt(kb, k) * decay, 0.0)
        inv = _unit_lower_inverse(low, row, col)
        sol = _bdot(inv, jnp.concatenate([v * beta, kb * egc], axis=-1))
        qk = _bdot_nt(q, k) * decay
        k_dec = k * jnp.exp(gcl - gc)
        kt = _bdot_tn(k_dec, sol)
        qs = _bdot(qk, sol)
        lhs = jnp.concatenate([kt[:, :, HEAD_DIM:], q * egc - qs[:, :, HEAD_DIM:]], axis=1)
        return lhs, kt[:, :, :HEAD_DIM], qs[:, :, :HEAD_DIM], jnp.exp(gcl)

    per = chunks // GDN_SPLIT
    local = [chunk_local(g * per, (g + 1) * per) for g in range(GDN_SPLIT)]
    for buf_ref in (xq_ref, xk_ref, xv_ref):
        buf_ref[0:SUBLANES, :] = buf_ref[rows:rows + SUBLANES, :]

    onw = onw_ref[...]
    states = [s_ref[hd] for hd in range(heads)]
    for c in range(chunks):
        r0 = c * CHUNK
        for hd in range(heads):
            lhs, add_s, add_o, carry = local[c // per]
            i = hd * per + c % per
            lanes = slice(hd * HEAD_DIM, (hd + 1) * HEAD_DIM)
            prod = _dot(lhs[i], states[hd])
            states[hd] = states[hd] * carry[i] - prod[:HEAD_DIM] + add_s[i]
            o = prod[HEAD_DIM:] + add_o[i]
            zg = _silu(x_ref[r0:r0 + CHUNK, 3 * width + lanes.start:3 * width + lanes.stop].astype(F32))
            o = o * lax.rsqrt(jnp.mean(o * o, axis=-1, keepdims=True) + EPS) * onw
            o_ref[r0:r0 + CHUNK, lanes] = (o * zg).astype(o_ref.dtype)
    for hd in range(heads):
        s_ref[hd] = states[hd]


def _gdn(proj, gates_t, conv_w, a_log, dt_bias, out_norm_w, batch, seq):
    rows, heads = GDN_ROWS, GDN_HEADS
    assert seq % rows == 0 and HEADS % heads == 0
    nt, groups, width = seq // rows, HEADS // heads, heads * HEAD_DIM
    cw = lambda off: pl.BlockSpec((CONV_K, width), lambda b, h, t: (0, off * groups + h))
    gate = lambda off: pl.BlockSpec((None, heads, 1, rows), lambda b, h, t: (b, off * groups + h, 0, t))
    smem = pl.BlockSpec(memory_space=pltpu.SMEM)
    return pl.pallas_call(
        _gdn_body,
        name="gdn",
        grid=(batch, groups, nt),
        in_specs=[smem, smem,
                  pl.BlockSpec((rows, 4 * width), lambda b, h, t: (b * nt + t, h)),
                  cw(0), cw(1), cw(2),
                  gate(0), gate(1),
                  pl.BlockSpec((1, HEAD_DIM), lambda b, h, t: (0, 0))],
        out_specs=pl.BlockSpec((rows, width), lambda b, h, t: (b * nt + t, h)),
        out_shape=jax.ShapeDtypeStruct((batch * seq, INNER), BF16),
        scratch_shapes=[pltpu.VMEM((rows + SUBLANES, width), F32)] * 3
                       + [pltpu.VMEM((heads, HEAD_DIM, HEAD_DIM), F32)],
        compiler_params=pltpu.CompilerParams(
            dimension_semantics=("parallel", "parallel", "arbitrary"), vmem_limit_bytes=VMEM_LIMIT),
    )(a_log, dt_bias, proj, conv_w, conv_w, conv_w, gates_t, gates_t,
      out_norm_w.reshape(1, HEAD_DIM))


def _attn_body(x_ref, qnw_ref, knw_ref, bias_ref, o_ref, kbuf_ref, vbuf_ref, s_ref):
    rows, width = o_ref.shape
    step = pl.program_id(2)

    @pl.when(step == 0)
    def _():
        kbuf_ref[0:LEFT, :] = jnp.zeros((LEFT, width), BF16)
        vbuf_ref[0:LEFT, :] = jnp.zeros((LEFT, width), BF16)

    def rms(x, w):
        return x * lax.rsqrt(jnp.mean(x * x, axis=-1, keepdims=True) + EPS) * w

    vbuf_ref[LEFT:LEFT + rows, :] = x_ref[:, 2 * width:3 * width]
    first_key = step * rows - LEFT
    col = lax.broadcasted_iota(jnp.int32, (1, BAND_PAD), 1)
    for hd in range(width // HEAD_DIM):
        lanes = slice(hd * HEAD_DIM, (hd + 1) * HEAD_DIM)
        q_lanes, k_lanes, z_lanes = (slice(part * width + lanes.start, part * width + lanes.stop)
                                     for part in (0, 1, 3))
        q = rms(x_ref[:, q_lanes].astype(F32), qnw_ref[...] * (HEAD_DIM ** -0.5 * LOG2E))
        kbuf_ref[LEFT:LEFT + rows, lanes] = rms(x_ref[:, k_lanes].astype(F32), knw_ref[...]).astype(BF16)
        for lo in range(0, rows, LANES):
            s_ref[hd, lo:lo + LANES, :] = _dot_nt(q[lo:lo + LANES], kbuf_ref[lo:lo + BAND_PAD, lanes])
        for c in range(rows // CHUNK):
            r0 = c * CHUNK
            lo = (c // 2) * LANES
            before_start = jnp.where(first_key + lo + col >= 0, 0.0, NEG)
            s = s_ref[hd, r0:r0 + CHUNK, :] + bias_ref[hd, c % 2] + before_start
            p = jnp.exp2(s - jnp.max(s, axis=-1, keepdims=True))
            denom = jnp.sum(p, axis=-1, keepdims=True)
            o = _dot(p, vbuf_ref[lo:lo + BAND_PAD, lanes]) / denom
            o = o * _silu(x_ref[r0:r0 + CHUNK, z_lanes].astype(F32))
            o_ref[r0:r0 + CHUNK, lanes] = o.astype(o_ref.dtype)

    kbuf_ref[0:LEFT, :] = kbuf_ref[rows:rows + LEFT, :]
    vbuf_ref[0:LEFT, :] = vbuf_ref[rows:rows + LEFT, :]


def _attn(proj, q_norm_w, k_norm_w, bias_tab, batch, seq):
    rows, heads = ATT_ROWS, ATT_HEADS
    assert seq % rows == 0 and rows >= LEFT and rows % LANES == 0 and HEADS % heads == 0
    nt, groups, width = seq // rows, HEADS // heads, heads * HEAD_DIM
    vec = pl.BlockSpec((1, HEAD_DIM), lambda b, h, t: (0, 0))
    return pl.pallas_call(
        _attn_body,
        name="band_attn",
        grid=(batch, groups, nt),
        in_specs=[pl.BlockSpec((rows, 4 * width), lambda b, h, t: (b * nt + t, h)), vec, vec,
                  pl.BlockSpec((heads, 2, CHUNK, BAND_PAD), lambda b, h, t: (h, 0, 0, 0))],
        out_specs=pl.BlockSpec((rows, width), lambda b, h, t: (b * nt + t, h)),
        out_shape=jax.ShapeDtypeStruct((batch * seq, INNER), BF16),
        scratch_shapes=[pltpu.VMEM((LEFT + rows, width), BF16),
                        pltpu.VMEM((LEFT + rows, width), BF16),
                        pltpu.VMEM((heads, rows, BAND_PAD), F32)],
        compiler_params=pltpu.CompilerParams(
            dimension_semantics=("parallel", "parallel", "arbitrary"), vmem_limit_bytes=VMEM_LIMIT),
    )(proj, q_norm_w.reshape(1, HEAD_DIM), k_norm_w.reshape(1, HEAD_DIM), bias_tab)


def _bias_table(rel_bias):
    t = jnp.arange(BAND + CHUNK - 1)
    idx = jnp.clip(LEFT + CHUNK - 1 - t, -REL_CLIP, REL_CLIP) + REL_CLIP
    diag = rel_bias.astype(F32)[:, idx] * LOG2E
    band = jnp.stack([diag[:, CHUNK - 1 - r:CHUNK - 1 - r + BAND] for r in range(CHUNK)], axis=1)
    pad = jnp.full((HEADS, CHUNK, CHUNK), NEG, F32)
    even = jnp.concatenate([band, pad], axis=-1)
    odd = jnp.concatenate([pad, band], axis=-1)
    return jnp.stack([even, odd], axis=1)


def kernel(x, norm_w, a_w_in, a_conv_w, a_a_log, a_dt_bias, a_out_norm_w, a_w_out,
           b_w_in, b_q_norm_w, b_k_norm_w, b_rel_bias, b_w_out):
    batch, seq, d = x.shape
    h0 = x.reshape(batch * seq, d)

    w_in = a_w_in[0]
    w_tail = jnp.pad(w_in[:, 4 * INNER:], ((0, 0), (0, LANES - 2 * HEADS))).astype(BF16)
    proj, tail = _rms_proj(h0, norm_w[0], w_in.astype(BF16), 4 * INNER, w_tail,
                           slabs=4, tn=4 * GDN_HEADS * HEAD_DIM)
    gates_t = tail[:, :2 * HEADS].reshape(batch, seq, 2 * HEADS).transpose(0, 2, 1)
    gates_t = gates_t.reshape(batch, 2 * HEADS, 1, seq)
    mixed = _gdn(proj, gates_t, a_conv_w[0], a_a_log[0], a_dt_bias[0], a_out_norm_w[0], batch, seq)
    h1 = _out_proj(mixed, a_w_out[0].astype(BF16), h0)

    proj = _rms_proj(h1, norm_w[1], b_w_in[0].astype(BF16), 4 * INNER, slabs=4, tn=4 * ATT_HEADS * HEAD_DIM)
    mixed = _attn(proj, b_q_norm_w[0], b_k_norm_w[0], _bias_table(b_rel_bias[0]), batch, seq)
    h2 = _out_proj(mixed, b_w_out[0].astype(BF16), h1)
    return h2.reshape(batch, seq, d)
```

```python
import functools

import jax
import jax.numpy as jnp
from jax import lax
from jax.experimental import pallas as pl
from jax.experimental.pallas import tpu as pltpu

F32 = jnp.float32
BF16 = jnp.bfloat16

EPS = 1e-6
CHUNK = 64
HEADS = 16
HEAD_DIM = 128
INNER = HEADS * HEAD_DIM
CONV_K = 4
LEFT_CHUNKS = 8
LEFT = LEFT_CHUNKS * CHUNK
BAND = LEFT + CHUNK
REL_CLIP = 256
NEG = -1e30
LOG2E = 1.4426950408889634

LANES = 128
SUBLANES = 8
VMEM_LIMIT = 54 * 1024 * 1024

GDN_ROWS = 512
GDN_HEADS = 4
GDN_SPLIT = 1
ATT_ROWS = 1024
ATT_HEADS = 4
BAND_PAD = BAND + CHUNK


def _dot(a, b):
    return jnp.dot(a.astype(BF16), b.astype(BF16), preferred_element_type=F32)


def _dot_nt(a, b):
    return lax.dot_general(a.astype(BF16), b.astype(BF16), (((1,), (1,)), ((), ())),
                           preferred_element_type=F32)


def _silu(x):
    return x * jax.nn.sigmoid(x)


def _softplus(x):
    return jnp.maximum(x, 0.0) + jnp.log1p(jnp.exp(-jnp.abs(x)))


def _rms_proj_body(x_ref, nw_ref, *rest, slabs, has_tail):
    w_refs, rest = rest[:slabs], rest[slabs:]
    if has_tail:
        wt_ref, o_ref, ot_ref, hn_ref = rest
    else:
        o_ref, hn_ref = rest

    @pl.when(pl.program_id(1) == 0)
    def _():
        x = x_ref[...]
        ms = jnp.mean(x * x, axis=-1, keepdims=True)
        hn = (x * lax.rsqrt(ms + EPS) * nw_ref[...]).astype(BF16)
        hn_ref[...] = hn
        if has_tail:
            ot_ref[...] = jnp.dot(hn, wt_ref[...], preferred_element_type=F32)

    width = o_ref.shape[1] // slabs
    for p, w_ref in enumerate(w_refs):
        o_ref[:, p * width:(p + 1) * width] = jnp.dot(
            hn_ref[...], w_ref[...], preferred_element_type=F32).astype(o_ref.dtype)


def _rms_proj(x, nw, w, n, w_tail=None, *, slabs=1, tm=1024, tn=2048):
    m, d = x.shape
    assert m % tm == 0 and n % tn == 0 and n <= w.shape[1] and tn % slabs == 0
    has_tail = w_tail is not None
    col_tiles = n // tn
    in_specs = [pl.BlockSpec((tm, d), lambda i, j: (i, 0)),
                pl.BlockSpec((1, d), lambda i, j: (0, 0))]
    in_specs += [pl.BlockSpec((d, tn // slabs), functools.partial(lambda i, j, p: (0, p * col_tiles + j), p=p))
                 for p in range(slabs)]
    out_specs = [pl.BlockSpec((tm, tn), lambda i, j: (i, j))]
    out_shape = [jax.ShapeDtypeStruct((m, n), BF16)]
    args = [x, nw.reshape(1, d)] + [w] * slabs
    if has_tail:
        nt = w_tail.shape[1]
        in_specs.append(pl.BlockSpec((d, nt), lambda i, j: (0, 0)))
        out_specs.append(pl.BlockSpec((tm, nt), lambda i, j: (i, 0)))
        out_shape.append(jax.ShapeDtypeStruct((m, nt), F32))
        args.append(w_tail)
    res = pl.pallas_call(
        functools.partial(_rms_proj_body, slabs=slabs, has_tail=has_tail),
        name="rms_proj_tail" if has_tail else "rms_proj",
        grid=(m // tm, col_tiles),
        in_specs=in_specs, out_specs=out_specs, out_shape=out_shape,
        scratch_shapes=[pltpu.VMEM((tm, d), BF16)],
        compiler_params=pltpu.CompilerParams(
            dimension_semantics=("parallel", "arbitrary"), vmem_limit_bytes=VMEM_LIMIT),
    )(*args)
    return res if has_tail else res[0]


def _out_proj_body(a_ref, w_ref, r_ref, o_ref):
    o_ref[...] = r_ref[...] + jnp.dot(a_ref[...], w_ref[...], preferred_element_type=F32)


def _out_proj(a, w, res, *, tm=512):
    m, k = a.shape
    n = w.shape[1]
    assert m % tm == 0
    return pl.pallas_call(
        _out_proj_body,
        name="out_proj",
        grid=(m // tm,),
        in_specs=[pl.BlockSpec((tm, k), lambda i: (i, 0)),
                  pl.BlockSpec((k, n), lambda i: (0, 0)),
                  pl.BlockSpec((tm, n), lambda i: (i, 0))],
        out_specs=pl.BlockSpec((tm, n), lambda i: (i, 0)),
        out_shape=jax.ShapeDtypeStruct((m, n), F32),
        compiler_params=pltpu.CompilerParams(
            dimension_semantics=("parallel",), vmem_limit_bytes=VMEM_LIMIT),
    )(a, w, res)


def _bdot(a, b):
    return jnp.einsum("cij,cjk->cik", a.astype(BF16), b.astype(BF16), preferred_element_type=F32)


def _bdot_nt(a, b):
    return jnp.einsum("cid,cjd->cij", a.astype(BF16), b.astype(BF16), preferred_element_type=F32)


def _bdot_tn(a, b):
    return jnp.einsum("cjd,cjn->cdn", a.astype(BF16), b.astype(BF16), preferred_element_type=F32)


def _unit_lower_inverse(low, row, col):
    eye = (row == col).astype(F32)
    base = SUBLANES
    ld = jnp.where((row // base) == (col // base), low, 0.0)
    l2 = _bdot(ld, ld)
    l4 = _bdot(l2, l2)
    x = eye - ld
    x = x + _bdot(x, l2)
    x = x + _bdot(x, l4)
    size = base
    low_b = low.astype(BF16)
    while size < CHUNK:
        pair = 2 * size
        off = ((row // pair) == (col // pair)) & ((row % pair) >= size) & ((col % pair) < size)
        c = jnp.where(off, low_b, jnp.zeros_like(low_b))
        x = x - _bdot(x, _bdot(c, x))
        size = pair
    return x


def _gdn_body(alog_ref, dtb_ref, x_ref, cwq_ref, cwk_ref, cwv_ref,
              a_ref, b_ref, onw_ref, o_ref, xq_ref, xk_ref, xv_ref, s_ref):
    rows, width = o_ref.shape
    heads = width // HEAD_DIM
    chunks = rows // CHUNK
    head0 = pl.program_id(1) * heads
    step = pl.program_id(2)

    @pl.when(step == 0)
    def _():
        zeros = jnp.zeros((SUBLANES, width), F32)
        xq_ref[0:SUBLANES, :] = zeros
        xk_ref[0:SUBLANES, :] = zeros
        xv_ref[0:SUBLANES, :] = zeros
        s_ref[...] = jnp.zeros_like(s_ref)

    for part, buf_ref in enumerate((xq_ref, xk_ref, xv_ref)):
        buf_ref[SUBLANES:SUBLANES + rows, :] = x_ref[:, part * width:(part + 1) * width].astype(F32)

    pos = lax.broadcasted_iota(jnp.int32, (SUBLANES, rows), 1) % CHUNK
    sub = lax.broadcasted_iota(jnp.int32, (SUBLANES, HEAD_DIM), 0)
    row = lax.broadcasted_iota(jnp.int32, (CHUNK, CHUNK), 0)
    col = lax.broadcasted_iota(jnp.int32, (CHUNK, CHUNK), 1)

    def gates(hd):
        a_in = jnp.broadcast_to(a_ref[hd], (SUBLANES, rows))
        b_in = jnp.broadcast_to(b_ref[hd], (SUBLANES, rows))
        neg_rate = -jnp.exp(jnp.full((SUBLANES, rows), alog_ref[head0 + hd], F32))
        gc = neg_rate * _softplus(a_in + dtb_ref[head0 + hd])
        beta = jax.nn.sigmoid(b_in)
        shift = 1
        while shift < CHUNK:
            gc = gc + jnp.where(pos >= shift, pltpu.roll(gc, shift, 1), 0.0)
            shift *= 2
        return (jnp.transpose(beta[0:1, :]).reshape(chunks, CHUNK, 1),
                jnp.transpose(gc[0:1, :]).reshape(chunks, CHUNK, 1),
                jnp.stack([gc[0:1, c * CHUNK:(c + 1) * CHUNK] for c in range(chunks)]))

    def conv_silu(buf_ref, cw_ref, hd, c0, c1):
        lanes = slice(hd * HEAD_DIM, (hd + 1) * HEAD_DIM)
        taps = [jnp.broadcast_to(cw_ref[j:j + 1, lanes], (SUBLANES, HEAD_DIM)) for j in range(CONV_K)]
        tiles = []
        for c in range(c0, c1):
            ext = buf_ref[c * CHUNK:(c + 1) * CHUNK + SUBLANES, lanes]
            ext = ext.reshape(CHUNK // SUBLANES + 1, SUBLANES, HEAD_DIM)
            y = ext[1:] * taps[CONV_K - 1]
            for shift in range(1, CONV_K):
                rot = pltpu.roll(ext, shift, 1)
                y = y + jnp.where(sub < shift, rot[:-1], rot[1:]) * taps[CONV_K - 1 - shift]
            tiles.append(_silu(y).reshape(CHUNK, HEAD_DIM))
        return jnp.stack(tiles)

    head_gates = [gates(hd) for hd in range(heads)]

    def chunk_local(c0, c1):
        parts = {name: [] for name in ("q", "k", "v", "beta", "gc", "gcr")}
        for hd in range(heads):
            beta, gc, gcr = head_gates[hd]
            q = conv_silu(xq_ref, cwq_ref, hd, c0, c1)
            k = conv_silu(xk_ref, cwk_ref, hd, c0, c1)
            parts["q"].append(q * lax.rsqrt(jnp.sum(q * q, axis=-1, keepdims=True) + EPS) * (HEAD_DIM ** -0.5))
            parts["k"].append(k * lax.rsqrt(jnp.sum(k * k, axis=-1, keepdims=True) + EPS))
            parts["v"].append(conv_silu(xv_ref, cwv_ref, hd, c0, c1))
            parts["beta"].append(beta[c0:c1])
            parts["gc"].append(gc[c0:c1])
            parts["gcr"].append(gcr[c0:c1])
        q, k, v, beta, gc, gcr = (jnp.concatenate(parts[n], axis=0) for n in ("q", "k", "v", "beta", "gc", "gcr"))
        gcl = gc[:, CHUNK - 1:CHUNK, :]
        decay = jnp.exp(jnp.where(row >= col, gc - gcr, -jnp.inf))
        egc = jnp.exp(gc)
        kb = k * beta
        low = jnp.where(row > col, _bdot_nt(kb, k) * decay, 0.0)
        inv = _unit_lower_inverse(low, row, col)
        sol = _bdot(inv, jnp.concatenate([v * beta, kb * egc], axis=-1))
        qk = _bdot_nt(q, k) * decay
        k_dec = k * jnp.exp(gcl - gc)
        kt = _bdot_tn(k_dec, sol)
        qs = _bdot(qk, sol)
        lhs = jnp.concatenate([kt[:, :, HEAD_DIM:], q * egc - qs[:, :, HEAD_DIM:]], axis=1)
        return lhs, kt[:, :, :HEAD_DIM], qs[:, :, :HEAD_DIM], jnp.exp(gcl)

    per = chunks // GDN_SPLIT
    local = [chunk_local(g * per, (g + 1) * per) for g in range(GDN_SPLIT)]
    for buf_ref in (xq_ref, xk_ref, xv_ref):
        buf_ref[0:SUBLANES, :] = buf_ref[rows:rows + SUBLANES, :]

    onw = onw_ref[...]
    states = [s_ref[hd] for hd in range(heads)]
    for c in range(chunks):
        r0 = c * CHUNK
        for hd in range(heads):
            lhs, add_s, add_o, carry = local[c // per]
            i = hd * per + c % per
            lanes = slice(hd * HEAD_DIM, (hd + 1) * HEAD_DIM)
            prod = _dot(lhs[i], states[hd])
            states[hd] = states[hd] * carry[i] - prod[:HEAD_DIM] + add_s[i]
            o = prod[HEAD_DIM:] + add_o[i]
            zg = _silu(x_ref[r0:r0 + CHUNK, 3 * width + lanes.start:3 * width + lanes.stop].astype(F32))
            o = o * lax.rsqrt(jnp.mean(o * o, axis=-1, keepdims=True) + EPS) * onw
            o_ref[r0:r0 + CHUNK, lanes] = (o * zg).astype(o_ref.dtype)
    for hd in range(heads):
        s_ref[hd] = states[hd]


def _gdn(proj, gates_t, conv_w, a_log, dt_bias, out_norm_w, batch, seq):
    rows, heads = GDN_ROWS, GDN_HEADS
    assert seq % rows == 0 and HEADS % heads == 0
    nt, groups, width = seq // rows, HEADS // heads, heads * HEAD_DIM
    cw = lambda off: pl.BlockSpec((CONV_K, width), lambda b, h, t: (0, off * groups + h))
    gate = lambda off: pl.BlockSpec((None, heads, 1, rows), lambda b, h, t: (b, off * groups + h, 0, t))
    smem = pl.BlockSpec(memory_space=pltpu.SMEM)
    return pl.pallas_call(
        _gdn_body,
        name="gdn",
        grid=(batch, groups, nt),
        in_specs=[smem, smem,
                  pl.BlockSpec((rows, 4 * width), lambda b, h, t: (b * nt + t, h)),
                  cw(0), cw(1), cw(2),
                  gate(0), gate(1),
                  pl.BlockSpec((1, HEAD_DIM), lambda b, h, t: (0, 0))],
        out_specs=pl.BlockSpec((rows, width), lambda b, h, t: (b * nt + t, h)),
        out_shape=jax.ShapeDtypeStruct((batch * seq, INNER), BF16),
        scratch_shapes=[pltpu.VMEM((rows + SUBLANES, width), F32)] * 3
                       + [pltpu.VMEM((heads, HEAD_DIM, HEAD_DIM), F32)],
        compiler_params=pltpu.CompilerParams(
            dimension_semantics=("parallel", "parallel", "arbitrary"), vmem_limit_bytes=VMEM_LIMIT),
    )(a_log, dt_bias, proj, conv_w, conv_w, conv_w, gates_t, gates_t,
      out_norm_w.reshape(1, HEAD_DIM))


def _attn_body(x_ref, qnw_ref, knw_ref, bias_ref, o_ref, kbuf_ref, vbuf_ref, s_ref):
    rows, width = o_ref.shape
    step = pl.program_id(2)

    def rms(x, w):
        return x * lax.rsqrt(jnp.mean(x * x, axis=-1, keepdims=True) + EPS) * w

    col = lax.broadcasted_iota(jnp.int32, (1, BAND_PAD), 1)

    def run(first_step):
        if first_step:
            kbuf_ref[0:LEFT, :] = jnp.zeros((LEFT, width), BF16)
            vbuf_ref[0:LEFT, :] = jnp.zeros((LEFT, width), BF16)
        vbuf_ref[LEFT:LEFT + rows, :] = x_ref[:, 2 * width:3 * width]
        for hd in range(width // HEAD_DIM):
            lanes = slice(hd * HEAD_DIM, (hd + 1) * HEAD_DIM)
            q_lanes, k_lanes, z_lanes = (slice(part * width + lanes.start, part * width + lanes.stop)
                                         for part in (0, 1, 3))
            q = rms(x_ref[:, q_lanes].astype(F32), qnw_ref[...] * (HEAD_DIM ** -0.5 * LOG2E))
            kbuf_ref[LEFT:LEFT + rows, lanes] = rms(x_ref[:, k_lanes].astype(F32), knw_ref[...]).astype(BF16)
            for lo in range(0, rows, LANES):
                s_ref[hd, lo:lo + LANES, :] = _dot_nt(q[lo:lo + LANES], kbuf_ref[lo:lo + BAND_PAD, lanes])
            for c in range(rows // CHUNK):
                r0 = c * CHUNK
                lo = (c // 2) * LANES
                s = s_ref[hd, r0:r0 + CHUNK, :] + bias_ref[hd, c % 2]
                if first_step and lo < LEFT:
                    s = s + jnp.where(lo + col >= LEFT, 0.0, NEG)
                p = jnp.exp2(s - jnp.max(s, axis=-1, keepdims=True))
                denom = jnp.sum(p, axis=-1, keepdims=True)
                o = _dot(p, vbuf_ref[lo:lo + BAND_PAD, lanes]) / denom
                o = o * _silu(x_ref[r0:r0 + CHUNK, z_lanes].astype(F32))
                o_ref[r0:r0 + CHUNK, lanes] = o.astype(o_ref.dtype)
        kbuf_ref[0:LEFT, :] = kbuf_ref[rows:rows + LEFT, :]
        vbuf_ref[0:LEFT, :] = vbuf_ref[rows:rows + LEFT, :]

    @pl.when(step == 0)
    def _():
        run(True)

    @pl.when(step > 0)
    def _():
        run(False)


def _attn(proj, q_norm_w, k_norm_w, bias_tab, batch, seq):
    rows, heads = ATT_ROWS, ATT_HEADS
    assert seq % rows == 0 and rows >= LEFT and rows % LANES == 0 and HEADS % heads == 0
    nt, groups, width = seq // rows, HEADS // heads, heads * HEAD_DIM
    vec = pl.BlockSpec((1, HEAD_DIM), lambda b, h, t: (0, 0))
    return pl.pallas_call(
        _attn_body,
        name="band_attn",
        grid=(batch, groups, nt),
        in_specs=[pl.BlockSpec((rows, 4 * width), lambda b, h, t: (b * nt + t, h)), vec, vec,
                  pl.BlockSpec((heads, 2, CHUNK, BAND_PAD), lambda b, h, t: (h, 0, 0, 0))],
        out_specs=pl.BlockSpec((rows, width), lambda b, h, t: (b * nt + t, h)),
        out_shape=jax.ShapeDtypeStruct((batch * seq, INNER), BF16),
        scratch_shapes=[pltpu.VMEM((LEFT + rows, width), BF16),
                        pltpu.VMEM((LEFT + rows, width), BF16),
                        pltpu.VMEM((heads, rows, BAND_PAD), F32)],
        compiler_params=pltpu.CompilerParams(
            dimension_semantics=("parallel", "parallel", "arbitrary"), vmem_limit_bytes=VMEM_LIMIT),
    )(proj, q_norm_w.reshape(1, HEAD_DIM), k_norm_w.reshape(1, HEAD_DIM), bias_tab)


def _bias_table(rel_bias):
    t = jnp.arange(BAND + CHUNK - 1)
    idx = jnp.clip(LEFT + CHUNK - 1 - t, -REL_CLIP, REL_CLIP) + REL_CLIP
    diag = rel_bias.astype(F32)[:, idx] * LOG2E
    band = jnp.stack([diag[:, CHUNK - 1 - r:CHUNK - 1 - r + BAND] for r in range(CHUNK)], axis=1)
    pad = jnp.full((HEADS, CHUNK, CHUNK), NEG, F32)
    even = jnp.concatenate([band, pad], axis=-1)
    odd = jnp.concatenate([pad, band], axis=-1)
    return jnp.stack([even, odd], axis=1)


def kernel(x, norm_w, a_w_in, a_conv_w, a_a_log, a_dt_bias, a_out_norm_w, a_w_out,
           b_w_in, b_q_norm_w, b_k_norm_w, b_rel_bias, b_w_out):
    batch, seq, d = x.shape
    h0 = x.reshape(batch * seq, d)

    w_in = a_w_in[0]
    w_tail = jnp.pad(w_in[:, 4 * INNER:], ((0, 0), (0, LANES - 2 * HEADS))).astype(BF16)
    proj, tail = _rms_proj(h0, norm_w[0], w_in.astype(BF16), 4 * INNER, w_tail,
                           slabs=4, tn=4 * GDN_HEADS * HEAD_DIM)
    gates_t = tail[:, :2 * HEADS].reshape(batch, seq, 2 * HEADS).transpose(0, 2, 1)
    gates_t = gates_t.reshape(batch, 2 * HEADS, 1, seq)
    mixed = _gdn(proj, gates_t, a_conv_w[0], a_a_log[0], a_dt_bias[0], a_out_norm_w[0], batch, seq)
    h1 = _out_proj(mixed, a_w_out[0].astype(BF16), h0)

    proj = _rms_proj(h1, norm_w[1], b_w_in[0].astype(BF16), 4 * INNER, slabs=4, tn=4 * ATT_HEADS * HEAD_DIM)
    mixed = _attn(proj, b_q_norm_w[0], b_k_norm_w[0], _bias_table(b_rel_bias[0]), batch, seq)
    h2 = _out_proj(mixed, b_w_out[0].astype(BF16), h1)
    return h2.reshape(batch, seq, d)
```

```python
import functools

import jax
import jax.numpy as jnp
from jax import lax
from jax.experimental import pallas as pl
from jax.experimental.pallas import tpu as pltpu

F32 = jnp.float32
BF16 = jnp.bfloat16

EPS = 1e-6
CHUNK = 64
HEADS = 16
HEAD_DIM = 128
INNER = HEADS * HEAD_DIM
CONV_K = 4
LEFT_CHUNKS = 8
LEFT = LEFT_CHUNKS * CHUNK
BAND = LEFT + CHUNK
REL_CLIP = 256
NEG = -1e30
LOG2E = 1.4426950408889634

LANES = 128
SUBLANES = 8
VMEM_LIMIT = 54 * 1024 * 1024

GDN_ROWS = 512
GDN_HEADS = 4
GDN_SPLIT = 1
ATT_ROWS = 1024
ATT_HEADS = 4
BAND_PAD = BAND + CHUNK


def _dot(a, b):
    return jnp.dot(a.astype(BF16), b.astype(BF16), preferred_element_type=F32)


def _dot_nt(a, b):
    return lax.dot_general(a.astype(BF16), b.astype(BF16), (((1,), (1,)), ((), ())),
                           preferred_element_type=F32)


def _silu(x):
    return x * jax.nn.sigmoid(x)


def _softplus(x):
    return jnp.maximum(x, 0.0) + jnp.log1p(jnp.exp(-jnp.abs(x)))


def _rms_proj_body(x_ref, nw_ref, *rest, slabs, has_tail):
    w_refs, rest = rest[:slabs], rest[slabs:]
    if has_tail:
        wt_ref, o_ref, ot_ref, hn_ref = rest
    else:
        o_ref, hn_ref = rest

    @pl.when(pl.program_id(1) == 0)
    def _():
        x = x_ref[...]
        ms = jnp.mean(x * x, axis=-1, keepdims=True)
        hn = (x * lax.rsqrt(ms + EPS) * nw_ref[...]).astype(BF16)
        hn_ref[...] = hn
        if has_tail:
            ot_ref[...] = jnp.dot(hn, wt_ref[...], preferred_element_type=F32)

    width = o_ref.shape[1] // slabs
    for p, w_ref in enumerate(w_refs):
        o_ref[:, p * width:(p + 1) * width] = jnp.dot(
            hn_ref[...], w_ref[...], preferred_element_type=F32).astype(o_ref.dtype)


def _rms_proj(x, nw, w, n, w_tail=None, *, slabs=1, tm=1024, tn=2048):
    m, d = x.shape
    assert m % tm == 0 and n % tn == 0 and n <= w.shape[1] and tn % slabs == 0
    has_tail = w_tail is not None
    col_tiles = n // tn
    in_specs = [pl.BlockSpec((tm, d), lambda i, j: (i, 0)),
                pl.BlockSpec((1, d), lambda i, j: (0, 0))]
    in_specs += [pl.BlockSpec((d, tn // slabs), functools.partial(lambda i, j, p: (0, p * col_tiles + j), p=p))
                 for p in range(slabs)]
    out_specs = [pl.BlockSpec((tm, tn), lambda i, j: (i, j))]
    out_shape = [jax.ShapeDtypeStruct((m, n), BF16)]
    args = [x, nw.reshape(1, d)] + [w] * slabs
    if has_tail:
        nt = w_tail.shape[1]
        in_specs.append(pl.BlockSpec((d, nt), lambda i, j: (0, 0)))
        out_specs.append(pl.BlockSpec((tm, nt), lambda i, j: (i, 0)))
        out_shape.append(jax.ShapeDtypeStruct((m, nt), F32))
        args.append(w_tail)
    res = pl.pallas_call(
        functools.partial(_rms_proj_body, slabs=slabs, has_tail=has_tail),
        name="rms_proj_tail" if has_tail else "rms_proj",
        grid=(m // tm, col_tiles),
        in_specs=in_specs, out_specs=out_specs, out_shape=out_shape,
        scratch_shapes=[pltpu.VMEM((tm, d), BF16)],
        compiler_params=pltpu.CompilerParams(
            dimension_semantics=("parallel", "arbitrary"), vmem_limit_bytes=VMEM_LIMIT),
    )(*args)
    return res if has_tail else res[0]


def _out_proj_body(a_ref, w_ref, r_ref, o_ref, wb_ref):
    @pl.when(pl.program_id(0) == 0)
    def _():
        wb_ref[...] = w_ref[...].astype(BF16)

    o_ref[...] = r_ref[...] + jnp.dot(a_ref[...], wb_ref[...], preferred_element_type=F32)


def _out_proj(a, w, res, *, tm=512):
    m, k = a.shape
    n = w.shape[1]
    assert m % tm == 0
    return pl.pallas_call(
        _out_proj_body,
        name="out_proj",
        grid=(m // tm,),
        in_specs=[pl.BlockSpec((tm, k), lambda i: (i, 0)),
                  pl.BlockSpec((k, n), lambda i: (0, 0), pipeline_mode=pl.Buffered(1)),
                  pl.BlockSpec((tm, n), lambda i: (i, 0))],
        out_specs=pl.BlockSpec((tm, n), lambda i: (i, 0)),
        out_shape=jax.ShapeDtypeStruct((m, n), F32),
        scratch_shapes=[pltpu.VMEM((k, n), BF16)],
        compiler_params=pltpu.CompilerParams(
            dimension_semantics=("arbitrary",), vmem_limit_bytes=VMEM_LIMIT),
    )(a, w, res)


def _bdot(a, b):
    return jnp.einsum("cij,cjk->cik", a.astype(BF16), b.astype(BF16), preferred_element_type=F32)


def _bdot_nt(a, b):
    return jnp.einsum("cid,cjd->cij", a.astype(BF16), b.astype(BF16), preferred_element_type=F32)


def _bdot_tn(a, b):
    return jnp.einsum("cjd,cjn->cdn", a.astype(BF16), b.astype(BF16), preferred_element_type=F32)


def _unit_lower_inverse(low, row, col):
    eye = (row == col).astype(F32)
    base = SUBLANES
    ld = jnp.where((row // base) == (col // base), low, 0.0)
    l2 = _bdot(ld, ld)
    l4 = _bdot(l2, l2)
    x = eye - ld
    x = x + _bdot(x, l2)
    x = x + _bdot(x, l4)
    size = base
    low_b = low.astype(BF16)
    while size < CHUNK:
        pair = 2 * size
        off = ((row // pair) == (col // pair)) & ((row % pair) >= size) & ((col % pair) < size)
        c = jnp.where(off, low_b, jnp.zeros_like(low_b))
        x = x - _bdot(x, _bdot(c, x))
        size = pair
    return x


def _gdn_body(alog_ref, dtb_ref, x_ref, cwq_ref, cwk_ref, cwv_ref,
              a_ref, b_ref, onw_ref, o_ref, xq_ref, xk_ref, xv_ref, s_ref):
    rows, width = o_ref.shape
    heads = width // HEAD_DIM
    chunks = rows // CHUNK
    head0 = pl.program_id(1) * heads
    step = pl.program_id(2)

    @pl.when(step == 0)
    def _():
        zeros = jnp.zeros((SUBLANES, width), F32)
        xq_ref[0:SUBLANES, :] = zeros
        xk_ref[0:SUBLANES, :] = zeros
        xv_ref[0:SUBLANES, :] = zeros
        s_ref[...] = jnp.zeros_like(s_ref)

    for part, buf_ref in enumerate((xq_ref, xk_ref, xv_ref)):
        buf_ref[SUBLANES:SUBLANES + rows, :] = x_ref[:, part * width:(part + 1) * width].astype(F32)

    pos = lax.broadcasted_iota(jnp.int32, (SUBLANES, rows), 1) % CHUNK
    sub = lax.broadcasted_iota(jnp.int32, (SUBLANES, HEAD_DIM), 0)
    row = lax.broadcasted_iota(jnp.int32, (CHUNK, CHUNK), 0)
    col = lax.broadcasted_iota(jnp.int32, (CHUNK, CHUNK), 1)

    def gates(hd):
        a_in = jnp.broadcast_to(a_ref[hd], (SUBLANES, rows))
        b_in = jnp.broadcast_to(b_ref[hd], (SUBLANES, rows))
        neg_rate = -jnp.exp(jnp.full((SUBLANES, rows), alog_ref[head0 + hd], F32))
        gc = neg_rate * _softplus(a_in + dtb_ref[head0 + hd])
        beta = jax.nn.sigmoid(b_in)
        shift = 1
        while shift < CHUNK:
            gc = gc + jnp.where(pos >= shift, pltpu.roll(gc, shift, 1), 0.0)
            shift *= 2
        return (jnp.transpose(beta[0:1, :]).reshape(chunks, CHUNK, 1),
                jnp.transpose(gc[0:1, :]).reshape(chunks, CHUNK, 1),
                jnp.stack([gc[0:1, c * CHUNK:(c + 1) * CHUNK] for c in range(chunks)]))

    def conv_silu(buf_ref, cw_ref, hd, c0, c1):
        lanes = slice(hd * HEAD_DIM, (hd + 1) * HEAD_DIM)
        taps = [jnp.broadcast_to(cw_ref[j:j + 1, lanes], (SUBLANES, HEAD_DIM)) for j in range(CONV_K)]
        tiles = []
        for c in range(c0, c1):
            ext = buf_ref[c * CHUNK:(c + 1) * CHUNK + SUBLANES, lanes]
            ext = ext.reshape(CHUNK // SUBLANES + 1, SUBLANES, HEAD_DIM)
            y = ext[1:] * taps[CONV_K - 1]
            for shift in range(1, CONV_K):
                rot = pltpu.roll(ext, shift, 1)
                y = y + jnp.where(sub < shift, rot[:-1], rot[1:]) * taps[CONV_K - 1 - shift]
            tiles.append(_silu(y).reshape(CHUNK, HEAD_DIM))
        return jnp.stack(tiles)

    head_gates = [gates(hd) for hd in range(heads)]

    def chunk_local(c0, c1):
        parts = {name: [] for name in ("q", "k", "v", "beta", "gc", "gcr")}
        for hd in range(heads):
            beta, gc, gcr = head_gates[hd]
            q = conv_silu(xq_ref, cwq_ref, hd, c0, c1)
            k = conv_silu(xk_ref, cwk_ref, hd, c0, c1)
            parts["q"].append(q * lax.rsqrt(jnp.sum(q * q, axis=-1, keepdims=True) + EPS) * (HEAD_DIM ** -0.5))
            parts["k"].append(k * lax.rsqrt(jnp.sum(k * k, axis=-1, keepdims=True) + EPS))
            parts["v"].append(conv_silu(xv_ref, cwv_ref, hd, c0, c1))
            parts["beta"].append(beta[c0:c1])
            parts["gc"].append(gc[c0:c1])
            parts["gcr"].append(gcr[c0:c1])
        q, k, v, beta, gc, gcr = (jnp.concatenate(parts[n], axis=0) for n in ("q", "k", "v", "beta", "gc", "gcr"))
        gcl = gc[:, CHUNK - 1:CHUNK, :]
        decay = jnp.exp(jnp.where(row >= col, gc - gcr, -jnp.inf))
        egc = jnp.exp(gc)
        kb = k * beta
        low = jnp.where(row > col, _bdot_nt(kb, k) * decay, 0.0)
        inv = _unit_lower_inverse(low, row, col)
        sol = _bdot(inv, jnp.concatenate([v * beta, kb * egc], axis=-1))
        qk = _bdot_nt(q, k) * decay
        k_dec = k * jnp.exp(gcl - gc)
        kt = _bdot_tn(k_dec, sol)
        qs = _bdot(qk, sol)
        lhs = jnp.concatenate([kt[:, :, HEAD_DIM:], q * egc - qs[:, :, HEAD_DIM:]], axis=1)
        return lhs, kt[:, :, :HEAD_DIM], qs[:, :, :HEAD_DIM], jnp.exp(gcl)

    per = chunks // GDN_SPLIT
    local = [chunk_local(g * per, (g + 1) * per) for g in range(GDN_SPLIT)]
    for buf_ref in (xq_ref, xk_ref, xv_ref):
        buf_ref[0:SUBLANES, :] = buf_ref[rows:rows + SUBLANES, :]

    onw = onw_ref[...]
    states = [s_ref[hd] for hd in range(heads)]
    for c in range(chunks):
        r0 = c * CHUNK
        for hd in range(heads):
            lhs, add_s, add_o, carry = local[c // per]
            i = hd * per + c % per
            lanes = slice(hd * HEAD_DIM, (hd + 1) * HEAD_DIM)
            prod = _dot(lhs[i], states[hd])
            states[hd] = states[hd] * carry[i] - prod[:HEAD_DIM] + add_s[i]
            o = prod[HEAD_DIM:] + add_o[i]
            zg = _silu(x_ref[r0:r0 + CHUNK, 3 * width + lanes.start:3 * width + lanes.stop].astype(F32))
            o = o * lax.rsqrt(jnp.mean(o * o, axis=-1, keepdims=True) + EPS) * onw
            o_ref[r0:r0 + CHUNK, lanes] = (o * zg).astype(o_ref.dtype)
    for hd in range(heads):
        s_ref[hd] = states[hd]


def _gdn(proj, gates_t, conv_w, a_log, dt_bias, out_norm_w, batch, seq):
    rows, heads = GDN_ROWS, GDN_HEADS
    assert seq % rows == 0 and HEADS % heads == 0
    nt, groups, width = seq // rows, HEADS // heads, heads * HEAD_DIM
    cw = lambda off: pl.BlockSpec((CONV_K, width), lambda b, h, t: (0, off * groups + h))
    gate = lambda off: pl.BlockSpec((None, heads, 1, rows), lambda b, h, t: (b, off * groups + h, 0, t))
    smem = pl.BlockSpec(memory_space=pltpu.SMEM)
    return pl.pallas_call(
        _gdn_body,
        name="gdn",
        grid=(batch, groups, nt),
        in_specs=[smem, smem,
                  pl.BlockSpec((rows, 4 * width), lambda b, h, t: (b * nt + t, h)),
                  cw(0), cw(1), cw(2),
                  gate(0), gate(1),
                  pl.BlockSpec((1, HEAD_DIM), lambda b, h, t: (0, 0))],
        out_specs=pl.BlockSpec((rows, width), lambda b, h, t: (b * nt + t, h)),
        out_shape=jax.ShapeDtypeStruct((batch * seq, INNER), BF16),
        scratch_shapes=[pltpu.VMEM((rows + SUBLANES, width), F32)] * 3
                       + [pltpu.VMEM((heads, HEAD_DIM, HEAD_DIM), F32)],
        compiler_params=pltpu.CompilerParams(
            dimension_semantics=("parallel", "parallel", "arbitrary"), vmem_limit_bytes=VMEM_LIMIT),
    )(a_log, dt_bias, proj, conv_w, conv_w, conv_w, gates_t, gates_t,
      out_norm_w.reshape(1, HEAD_DIM))


def _attn_body(x_ref, qnw_ref, knw_ref, bias_ref, o_ref, kbuf_ref, vbuf_ref, s_ref):
    rows, width = o_ref.shape
    step = pl.program_id(2)

    def rms(x, w):
        return x * lax.rsqrt(jnp.mean(x * x, axis=-1, keepdims=True) + EPS) * w

    col = lax.broadcasted_iota(jnp.int32, (1, BAND_PAD), 1)

    def run(first_step):
        if first_step:
            kbuf_ref[0:LEFT, :] = jnp.zeros((LEFT, width), BF16)
            vbuf_ref[0:LEFT, :] = jnp.zeros((LEFT, width), BF16)
        vbuf_ref[LEFT:LEFT + rows, :] = x_ref[:, 2 * width:3 * width]
        for hd in range(width // HEAD_DIM):
            lanes = slice(hd * HEAD_DIM, (hd + 1) * HEAD_DIM)
            q_lanes, k_lanes, z_lanes = (slice(part * width + lanes.start, part * width + lanes.stop)
                                         for part in (0, 1, 3))
            q = rms(x_ref[:, q_lanes].astype(F32), qnw_ref[...] * (HEAD_DIM ** -0.5 * LOG2E))
            kbuf_ref[LEFT:LEFT + rows, lanes] = rms(x_ref[:, k_lanes].astype(F32), knw_ref[...]).astype(BF16)
            for lo in range(0, rows, LANES):
                s_ref[hd, lo:lo + LANES, :] = _dot_nt(q[lo:lo + LANES], kbuf_ref[lo:lo + BAND_PAD, lanes])
            for c in range(rows // CHUNK):
                r0 = c * CHUNK
                lo = (c // 2) * LANES
                s = s_ref[hd, r0:r0 + CHUNK, :] + bias_ref[hd, c % 2]
                if first_step and lo < LEFT:
                    s = s + jnp.where(lo + col >= LEFT, 0.0, NEG)
                p = jnp.exp2(s - jnp.max(s, axis=-1, keepdims=True))
                denom = jnp.sum(p, axis=-1, keepdims=True)
                o = _dot(p, vbuf_ref[lo:lo + BAND_PAD, lanes]) / denom
                o = o * _silu(x_ref[r0:r0 + CHUNK, z_lanes].astype(F32))
                o_ref[r0:r0 + CHUNK, lanes] = o.astype(o_ref.dtype)
        kbuf_ref[0:LEFT, :] = kbuf_ref[rows:rows + LEFT, :]
        vbuf_ref[0:LEFT, :] = vbuf_ref[rows:rows + LEFT, :]

    @pl.when(step == 0)
    def _():
        run(True)

    @pl.when(step > 0)
    def _():
        run(False)


def _attn(proj, q_norm_w, k_norm_w, bias_tab, batch, seq):
    rows, heads = ATT_ROWS, ATT_HEADS
    assert seq % rows == 0 and rows >= LEFT and rows % LANES == 0 and HEADS % heads == 0
    nt, groups, width = seq // rows, HEADS // heads, heads * HEAD_DIM
    vec = pl.BlockSpec((1, HEAD_DIM), lambda b, h, t: (0, 0))
    return pl.pallas_call(
        _attn_body,
        name="band_attn",
        grid=(batch, groups, nt),
        in_specs=[pl.BlockSpec((rows, 4 * width), lambda b, h, t: (b * nt + t, h)), vec, vec,
                  pl.BlockSpec((heads, 2, CHUNK, BAND_PAD), lambda b, h, t: (h, 0, 0, 0))],
        out_specs=pl.BlockSpec((rows, width), lambda b, h, t: (b * nt + t, h)),
        out_shape=jax.ShapeDtypeStruct((batch * seq, INNER), BF16),
        scratch_shapes=[pltpu.VMEM((LEFT + rows, width), BF16),
                        pltpu.VMEM((LEFT + rows, width), BF16),
                        pltpu.VMEM((heads, rows, BAND_PAD), F32)],
        compiler_params=pltpu.CompilerParams(
            dimension_semantics=("parallel", "parallel", "arbitrary"), vmem_limit_bytes=VMEM_LIMIT),
    )(proj, q_norm_w.reshape(1, HEAD_DIM), k_norm_w.reshape(1, HEAD_DIM), bias_tab)


def _bias_table(rel_bias):
    t = jnp.arange(BAND + CHUNK - 1)
    idx = jnp.clip(LEFT + CHUNK - 1 - t, -REL_CLIP, REL_CLIP) + REL_CLIP
    diag = rel_bias.astype(F32)[:, idx] * LOG2E
    band = jnp.stack([diag[:, CHUNK - 1 - r:CHUNK - 1 - r + BAND] for r in range(CHUNK)], axis=1)
    pad = jnp.full((HEADS, CHUNK, CHUNK), NEG, F32)
    even = jnp.concatenate([band, pad], axis=-1)
    odd = jnp.concatenate([pad, band], axis=-1)
    return jnp.stack([even, odd], axis=1)


def kernel(x, norm_w, a_w_in, a_conv_w, a_a_log, a_dt_bias, a_out_norm_w, a_w_out,
           b_w_in, b_q_norm_w, b_k_norm_w, b_rel_bias, b_w_out):
    batch, seq, d = x.shape
    h0 = x.reshape(batch * seq, d)

    w_in = a_w_in[0]
    w_tail = jnp.pad(w_in[:, 4 * INNER:], ((0, 0), (0, LANES - 2 * HEADS))).astype(BF16)
    proj, tail = _rms_proj(h0, norm_w[0], w_in.astype(BF16), 4 * INNER, w_tail,
                           slabs=4, tn=4 * GDN_HEADS * HEAD_DIM)
    gates_t = tail[:, :2 * HEADS].reshape(batch, seq, 2 * HEADS).transpose(0, 2, 1)
    gates_t = gates_t.reshape(batch, 2 * HEADS, 1, seq)
    mixed = _gdn(proj, gates_t, a_conv_w[0], a_a_log[0], a_dt_bias[0], a_out_norm_w[0], batch, seq)
    h1 = _out_proj(mixed, a_w_out[0], h0)

    proj = _rms_proj(h1, norm_w[1], b_w_in[0].astype(BF16), 4 * INNER, slabs=4, tn=4 * ATT_HEADS * HEAD_DIM)
    mixed = _attn(proj, b_q_norm_w[0], b_k_norm_w[0], _bias_table(b_rel_bias[0]), batch, seq)
    h2 = _out_proj(mixed, b_w_out[0], h1)
    return h2.reshape(batch, seq, d)
```

```python
import functools

import jax
import jax.numpy as jnp
from jax import lax
from jax.experimental import pallas as pl
from jax.experimental.pallas import tpu as pltpu

F32 = jnp.float32
BF16 = jnp.bfloat16

EPS = 1e-6
CHUNK = 64
HEADS = 16
HEAD_DIM = 128
INNER = HEADS * HEAD_DIM
CONV_K = 4
LEFT_CHUNKS = 8
LEFT = LEFT_CHUNKS * CHUNK
BAND = LEFT + CHUNK
REL_CLIP = 256
NEG = -1e30
LOG2E = 1.4426950408889634

LANES = 128
SUBLANES = 8
VMEM_LIMIT = 54 * 1024 * 1024

GDN_ROWS = 512
GDN_HEADS = 4
GDN_SPLIT = 1
ATT_ROWS = 1024
ATT_HEADS = 4
BAND_PAD = BAND + CHUNK


def _dot(a, b):
    return jnp.dot(a.astype(BF16), b.astype(BF16), preferred_element_type=F32)


def _dot_nt(a, b):
    return lax.dot_general(a.astype(BF16), b.astype(BF16), (((1,), (1,)), ((), ())),
                           preferred_element_type=F32)


def _silu(x):
    return x * jax.nn.sigmoid(x)


def _softplus(x):
    return jnp.maximum(x, 0.0) + jnp.log1p(jnp.exp(-jnp.abs(x)))


def _rms_proj_body(x_ref, nw_ref, *rest, slabs, has_tail, transposed):
    w_refs, rest = rest[:slabs], rest[slabs:]
    mm = _dot_nt if transposed else functools.partial(jnp.dot, preferred_element_type=F32)
    if has_tail:
        wt_ref, o_ref, ot_ref, hn_ref = rest
    else:
        o_ref, hn_ref = rest

    @pl.when(pl.program_id(1) == 0)
    def _():
        x = x_ref[...]
        ms = jnp.mean(x * x, axis=-1, keepdims=True)
        hn = (x * lax.rsqrt(ms + EPS) * nw_ref[...]).astype(BF16)
        hn_ref[...] = hn
        if has_tail:
            ot_ref[...] = mm(hn, wt_ref[...])

    width = o_ref.shape[1] // slabs
    for p, w_ref in enumerate(w_refs):
        o_ref[:, p * width:(p + 1) * width] = mm(hn_ref[...], w_ref[...]).astype(o_ref.dtype)


def _rms_proj(x, nw, w, n, w_tail=None, *, slabs=1, tm=1024, tn=2048, transposed=False):
    m, d = x.shape
    assert m % tm == 0 and n % tn == 0 and n <= w.shape[0 if transposed else 1] and tn % slabs == 0
    has_tail = w_tail is not None
    col_tiles = n // tn
    flip = (lambda shape: shape[::-1]) if transposed else (lambda shape: shape)
    in_specs = [pl.BlockSpec((tm, d), lambda i, j: (i, 0)),
                pl.BlockSpec((1, d), lambda i, j: (0, 0))]
    in_specs += [pl.BlockSpec(flip((d, tn // slabs)),
                              functools.partial(lambda i, j, p: flip((0, p * col_tiles + j)), p=p))
                 for p in range(slabs)]
    out_specs = [pl.BlockSpec((tm, tn), lambda i, j: (i, j))]
    out_shape = [jax.ShapeDtypeStruct((m, n), BF16)]
    args = [x, nw.reshape(1, d)] + [w] * slabs
    if has_tail:
        nt = w_tail.shape[0 if transposed else 1]
        in_specs.append(pl.BlockSpec(flip((d, nt)), lambda i, j: (0, 0)))
        out_specs.append(pl.BlockSpec((tm, nt), lambda i, j: (i, 0)))
        out_shape.append(jax.ShapeDtypeStruct((m, nt), F32))
        args.append(w_tail)
    res = pl.pallas_call(
        functools.partial(_rms_proj_body, slabs=slabs, has_tail=has_tail, transposed=transposed),
        name="rms_proj_tail" if has_tail else "rms_proj",
        grid=(m // tm, col_tiles),
        in_specs=in_specs, out_specs=out_specs, out_shape=out_shape,
        scratch_shapes=[pltpu.VMEM((tm, d), BF16)],
        compiler_params=pltpu.CompilerParams(
            dimension_semantics=("parallel", "arbitrary"), vmem_limit_bytes=VMEM_LIMIT),
    )(*args)
    return res if has_tail else res[0]


def _out_proj_body(a_ref, w_ref, r_ref, o_ref, wb_ref):
    @pl.when(pl.program_id(0) == 0)
    def _():
        wb_ref[...] = w_ref[...].astype(BF16)

    o_ref[...] = r_ref[...] + jnp.dot(a_ref[...], wb_ref[...], preferred_element_type=F32)


def _out_proj(a, w, res, *, tm=512):
    m, k = a.shape
    n = w.shape[1]
    assert m % tm == 0
    return pl.pallas_call(
        _out_proj_body,
        name="out_proj",
        grid=(m // tm,),
        in_specs=[pl.BlockSpec((tm, k), lambda i: (i, 0)),
                  pl.BlockSpec((k, n), lambda i: (0, 0), pipeline_mode=pl.Buffered(1)),
                  pl.BlockSpec((tm, n), lambda i: (i, 0))],
        out_specs=pl.BlockSpec((tm, n), lambda i: (i, 0)),
        out_shape=jax.ShapeDtypeStruct((m, n), F32),
        scratch_shapes=[pltpu.VMEM((k, n), BF16)],
        compiler_params=pltpu.CompilerParams(
            dimension_semantics=("arbitrary",), vmem_limit_bytes=VMEM_LIMIT),
    )(a, w, res)


def _bdot(a, b):
    return jnp.einsum("cij,cjk->cik", a.astype(BF16), b.astype(BF16), preferred_element_type=F32)


def _bdot_nt(a, b):
    return jnp.einsum("cid,cjd->cij", a.astype(BF16), b.astype(BF16), preferred_element_type=F32)


def _bdot_tn(a, b):
    return jnp.einsum("cjd,cjn->cdn", a.astype(BF16), b.astype(BF16), preferred_element_type=F32)


def _unit_lower_inverse(low, row, col):
    eye = (row == col).astype(F32)
    base = SUBLANES
    ld = jnp.where((row // base) == (col // base), low, 0.0)
    l2 = _bdot(ld, ld)
    l4 = _bdot(l2, l2)
    x = eye - ld
    x = x + _bdot(x, l2)
    x = x + _bdot(x, l4)
    size = base
    low_b = low.astype(BF16)
    while size < CHUNK:
        pair = 2 * size
        off = ((row // pair) == (col // pair)) & ((row % pair) >= size) & ((col % pair) < size)
        c = jnp.where(off, low_b, jnp.zeros_like(low_b))
        x = x - _bdot(x, _bdot(c, x))
        size = pair
    return x


def _gdn_body(alog_ref, dtb_ref, x_ref, cwq_ref, cwk_ref, cwv_ref,
              a_ref, b_ref, onw_ref, o_ref, xq_ref, xk_ref, xv_ref, s_ref):
    rows, width = o_ref.shape
    heads = width // HEAD_DIM
    chunks = rows // CHUNK
    head0 = pl.program_id(1) * heads
    step = pl.program_id(2)

    @pl.when(step == 0)
    def _():
        zeros = jnp.zeros((SUBLANES, width), F32)
        xq_ref[0:SUBLANES, :] = zeros
        xk_ref[0:SUBLANES, :] = zeros
        xv_ref[0:SUBLANES, :] = zeros
        s_ref[...] = jnp.zeros_like(s_ref)

    for part, buf_ref in enumerate((xq_ref, xk_ref, xv_ref)):
        buf_ref[SUBLANES:SUBLANES + rows, :] = x_ref[:, part * width:(part + 1) * width].astype(F32)

    pos = lax.broadcasted_iota(jnp.int32, (SUBLANES, rows), 1) % CHUNK
    sub = lax.broadcasted_iota(jnp.int32, (SUBLANES, HEAD_DIM), 0)
    row = lax.broadcasted_iota(jnp.int32, (CHUNK, CHUNK), 0)
    col = lax.broadcasted_iota(jnp.int32, (CHUNK, CHUNK), 1)

    def gates(hd):
        a_in = jnp.broadcast_to(a_ref[hd], (SUBLANES, rows))
        b_in = jnp.broadcast_to(b_ref[hd], (SUBLANES, rows))
        neg_rate = -jnp.exp(jnp.full((SUBLANES, rows), alog_ref[head0 + hd], F32))
        gc = neg_rate * _softplus(a_in + dtb_ref[head0 + hd])
        beta = jax.nn.sigmoid(b_in)
        shift = 1
        while shift < CHUNK:
            gc = gc + jnp.where(pos >= shift, pltpu.roll(gc, shift, 1), 0.0)
            shift *= 2
        return (jnp.transpose(beta[0:1, :]).reshape(chunks, CHUNK, 1),
                jnp.transpose(gc[0:1, :]).reshape(chunks, CHUNK, 1),
                jnp.stack([gc[0:1, c * CHUNK:(c + 1) * CHUNK] for c in range(chunks)]))

    def conv_silu(buf_ref, cw_ref, hd, c0, c1):
        lanes = slice(hd * HEAD_DIM, (hd + 1) * HEAD_DIM)
        taps = [jnp.broadcast_to(cw_ref[j:j + 1, lanes], (SUBLANES, HEAD_DIM)) for j in range(CONV_K)]
        tiles = []
        for c in range(c0, c1):
            ext = buf_ref[c * CHUNK:(c + 1) * CHUNK + SUBLANES, lanes]
            ext = ext.reshape(CHUNK // SUBLANES + 1, SUBLANES, HEAD_DIM)
            y = ext[1:] * taps[CONV_K - 1]
            for shift in range(1, CONV_K):
                rot = pltpu.roll(ext, shift, 1)
                y = y + jnp.where(sub < shift, rot[:-1], rot[1:]) * taps[CONV_K - 1 - shift]
            tiles.append(_silu(y).reshape(CHUNK, HEAD_DIM))
        return jnp.stack(tiles)

    head_gates = [gates(hd) for hd in range(heads)]

    def chunk_local(c0, c1):
        parts = {name: [] for name in ("q", "k", "v", "beta", "gc", "gcr")}
        for hd in range(heads):
            beta, gc, gcr = head_gates[hd]
            q = conv_silu(xq_ref, cwq_ref, hd, c0, c1)
            k = conv_silu(xk_ref, cwk_ref, hd, c0, c1)
            parts["q"].append(q * lax.rsqrt(jnp.sum(q * q, axis=-1, keepdims=True) + EPS) * (HEAD_DIM ** -0.5))
            parts["k"].append(k * lax.rsqrt(jnp.sum(k * k, axis=-1, keepdims=True) + EPS))
            parts["v"].append(conv_silu(xv_ref, cwv_ref, hd, c0, c1))
            parts["beta"].append(beta[c0:c1])
            parts["gc"].append(gc[c0:c1])
            parts["gcr"].append(gcr[c0:c1])
        q, k, v, beta, gc, gcr = (jnp.concatenate(parts[n], axis=0) for n in ("q", "k", "v", "beta", "gc", "gcr"))
        gcl = gc[:, CHUNK - 1:CHUNK, :]
        decay = jnp.exp(jnp.where(row >= col, gc - gcr, -jnp.inf))
        egc = jnp.exp(gc)
        kb = k * beta
        low = jnp.where(row > col, _bdot_nt(kb, k) * decay, 0.0)
        inv = _unit_lower_inverse(low, row, col)
        sol = _bdot(inv, jnp.concatenate([v * beta, kb * egc], axis=-1))
        qk = _bdot_nt(q, k) * decay
        k_dec = k * jnp.exp(gcl - gc)
        kt = _bdot_tn(k_dec, sol)
        qs = _bdot(qk, sol)
        lhs = jnp.concatenate([kt[:, :, HEAD_DIM:], q * egc - qs[:, :, HEAD_DIM:]], axis=1)
        return lhs, kt[:, :, :HEAD_DIM], qs[:, :, :HEAD_DIM], jnp.exp(gcl)

    per = chunks // GDN_SPLIT
    local = [chunk_local(g * per, (g + 1) * per) for g in range(GDN_SPLIT)]
    for buf_ref in (xq_ref, xk_ref, xv_ref):
        buf_ref[0:SUBLANES, :] = buf_ref[rows:rows + SUBLANES, :]

    onw = onw_ref[...]
    states = [s_ref[hd] for hd in range(heads)]
    for c in range(chunks):
        r0 = c * CHUNK
        for hd in range(heads):
            lhs, add_s, add_o, carry = local[c // per]
            i = hd * per + c % per
            lanes = slice(hd * HEAD_DIM, (hd + 1) * HEAD_DIM)
            prod = _dot(lhs[i], states[hd])
            states[hd] = states[hd] * carry[i] - prod[:HEAD_DIM] + add_s[i]
            o = prod[HEAD_DIM:] + add_o[i]
            zg = _silu(x_ref[r0:r0 + CHUNK, 3 * width + lanes.start:3 * width + lanes.stop].astype(F32))
            o = o * lax.rsqrt(jnp.mean(o * o, axis=-1, keepdims=True) + EPS) * onw
            o_ref[r0:r0 + CHUNK, lanes] = (o * zg).astype(o_ref.dtype)
    for hd in range(heads):
        s_ref[hd] = states[hd]


def _gdn(proj, gates_t, conv_w, a_log, dt_bias, out_norm_w, batch, seq):
    rows, heads = GDN_ROWS, GDN_HEADS
    assert seq % rows == 0 and HEADS % heads == 0
    nt, groups, width = seq // rows, HEADS // heads, heads * HEAD_DIM
    cw = lambda off: pl.BlockSpec((CONV_K, width), lambda b, h, t: (0, off * groups + h))
    gate = lambda off: pl.BlockSpec((None, heads, 1, rows), lambda b, h, t: (b, off * groups + h, 0, t))
    smem = pl.BlockSpec(memory_space=pltpu.SMEM)
    return pl.pallas_call(
        _gdn_body,
        name="gdn",
        grid=(batch, groups, nt),
        in_specs=[smem, smem,
                  pl.BlockSpec((rows, 4 * width), lambda b, h, t: (b * nt + t, h)),
                  cw(0), cw(1), cw(2),
                  gate(0), gate(1),
                  pl.BlockSpec((1, HEAD_DIM), lambda b, h, t: (0, 0))],
        out_specs=pl.BlockSpec((rows, width), lambda b, h, t: (b * nt + t, h)),
        out_shape=jax.ShapeDtypeStruct((batch * seq, INNER), BF16),
        scratch_shapes=[pltpu.VMEM((rows + SUBLANES, width), F32)] * 3
                       + [pltpu.VMEM((heads, HEAD_DIM, HEAD_DIM), F32)],
        compiler_params=pltpu.CompilerParams(
            dimension_semantics=("parallel", "parallel", "arbitrary"), vmem_limit_bytes=VMEM_LIMIT),
    )(a_log, dt_bias, proj, conv_w, conv_w, conv_w, gates_t, gates_t,
      out_norm_w.reshape(1, HEAD_DIM))


def _attn_body(x_ref, qnw_ref, knw_ref, bias_ref, o_ref, kbuf_ref, vbuf_ref, s_ref):
    rows, width = o_ref.shape
    step = pl.program_id(2)

    def rms(x, w):
        return x * lax.rsqrt(jnp.mean(x * x, axis=-1, keepdims=True) + EPS) * w

    col = lax.broadcasted_iota(jnp.int32, (1, BAND_PAD), 1)

    def run(first_step):
        if first_step:
            kbuf_ref[0:LEFT, :] = jnp.zeros((LEFT, width), BF16)
            vbuf_ref[0:LEFT, :] = jnp.zeros((LEFT, width), BF16)
        vbuf_ref[LEFT:LEFT + rows, :] = x_ref[:, 2 * width:3 * width]
        for hd in range(width // HEAD_DIM):
            lanes = slice(hd * HEAD_DIM, (hd + 1) * HEAD_DIM)
            q_lanes, k_lanes, z_lanes = (slice(part * width + lanes.start, part * width + lanes.stop)
                                         for part in (0, 1, 3))
            q = rms(x_ref[:, q_lanes].astype(F32), qnw_ref[...] * (HEAD_DIM ** -0.5 * LOG2E))
            kbuf_ref[LEFT:LEFT + rows, lanes] = rms(x_ref[:, k_lanes].astype(F32), knw_ref[...]).astype(BF16)
            for lo in range(0, rows, LANES):
                s_ref[hd, lo:lo + LANES, :] = _dot_nt(q[lo:lo + LANES], kbuf_ref[lo:lo + BAND_PAD, lanes])
            for c in range(rows // CHUNK):
                r0 = c * CHUNK
                lo = (c // 2) * LANES
                s = s_ref[hd, r0:r0 + CHUNK, :] + bias_ref[hd, c % 2]
                if first_step and lo < LEFT:
                    s = s + jnp.where(lo + col >= LEFT, 0.0, NEG)
                p = jnp.exp2(s - jnp.max(s, axis=-1, keepdims=True))
                denom = jnp.sum(p, axis=-1, keepdims=True)
                o = _dot(p, vbuf_ref[lo:lo + BAND_PAD, lanes]) / denom
                o = o * _silu(x_ref[r0:r0 + CHUNK, z_lanes].astype(F32))
                o_ref[r0:r0 + CHUNK, lanes] = o.astype(o_ref.dtype)
        kbuf_ref[0:LEFT, :] = kbuf_ref[rows:rows + LEFT, :]
        vbuf_ref[0:LEFT, :] = vbuf_ref[rows:rows + LEFT, :]

    @pl.when(step == 0)
    def _():
        run(True)

    @pl.when(step > 0)
    def _():
        run(False)


def _attn(proj, q_norm_w, k_norm_w, bias_tab, batch, seq):
    rows, heads = ATT_ROWS, ATT_HEADS
    assert seq % rows == 0 and rows >= LEFT and rows % LANES == 0 and HEADS % heads == 0
    nt, groups, width = seq // rows, HEADS // heads, heads * HEAD_DIM
    vec = pl.BlockSpec((1, HEAD_DIM), lambda b, h, t: (0, 0))
    return pl.pallas_call(
        _attn_body,
        name="band_attn",
        grid=(batch, groups, nt),
        in_specs=[pl.BlockSpec((rows, 4 * width), lambda b, h, t: (b * nt + t, h)), vec, vec,
                  pl.BlockSpec((heads, 2, CHUNK, BAND_PAD), lambda b, h, t: (h, 0, 0, 0))],
        out_specs=pl.BlockSpec((rows, width), lambda b, h, t: (b * nt + t, h)),
        out_shape=jax.ShapeDtypeStruct((batch * seq, INNER), BF16),
        scratch_shapes=[pltpu.VMEM((LEFT + rows, width), BF16),
                        pltpu.VMEM((LEFT + rows, width), BF16),
                        pltpu.VMEM((heads, rows, BAND_PAD), F32)],
        compiler_params=pltpu.CompilerParams(
            dimension_semantics=("parallel", "parallel", "arbitrary"), vmem_limit_bytes=VMEM_LIMIT),
    )(proj, q_norm_w.reshape(1, HEAD_DIM), k_norm_w.reshape(1, HEAD_DIM), bias_tab)


def _bias_table(rel_bias):
    t = jnp.arange(BAND + CHUNK - 1)
    idx = jnp.clip(LEFT + CHUNK - 1 - t, -REL_CLIP, REL_CLIP) + REL_CLIP
    diag = rel_bias.astype(F32)[:, idx] * LOG2E
    band = jnp.stack([diag[:, CHUNK - 1 - r:CHUNK - 1 - r + BAND] for r in range(CHUNK)], axis=1)
    pad = jnp.full((HEADS, CHUNK, CHUNK), NEG, F32)
    even = jnp.concatenate([band, pad], axis=-1)
    odd = jnp.concatenate([pad, band], axis=-1)
    return jnp.stack([even, odd], axis=1)


def kernel(x, norm_w, a_w_in, a_conv_w, a_a_log, a_dt_bias, a_out_norm_w, a_w_out,
           b_w_in, b_q_norm_w, b_k_norm_w, b_rel_bias, b_w_out):
    batch, seq, d = x.shape
    h0 = x.reshape(batch * seq, d)

    w_in = jnp.swapaxes(a_w_in[0], 0, 1)
    w_tail = jnp.pad(w_in[4 * INNER:], ((0, LANES - 2 * HEADS), (0, 0))).astype(BF16)
    proj, tail = _rms_proj(h0, norm_w[0], w_in.astype(BF16), 4 * INNER, w_tail,
                           slabs=4, tn=4 * GDN_HEADS * HEAD_DIM, transposed=True)
    gates_t = tail[:, :2 * HEADS].reshape(batch, seq, 2 * HEADS).transpose(0, 2, 1)
    gates_t = gates_t.reshape(batch, 2 * HEADS, 1, seq)
    mixed = _gdn(proj, gates_t, a_conv_w[0], a_a_log[0], a_dt_bias[0], a_out_norm_w[0], batch, seq)
    h1 = _out_proj(mixed, a_w_out[0], h0)

    proj = _rms_proj(h1, norm_w[1], b_w_in[0].astype(BF16), 4 * INNER, slabs=4, tn=4 * ATT_HEADS * HEAD_DIM)
    mixed = _attn(proj, b_q_norm_w[0], b_k_norm_w[0], _bias_table(b_rel_bias[0]), batch, seq)
    h2 = _out_proj(mixed, b_w_out[0], h1)
    return h2.reshape(batch, seq, d)
```

```python
import functools

import jax
import jax.numpy as jnp
from jax import lax
from jax.experimental import pallas as pl
from jax.experimental.pallas import tpu as pltpu

F32 = jnp.float32
BF16 = jnp.bfloat16

EPS = 1e-6
CHUNK = 64
HEADS = 16
HEAD_DIM = 128
INNER = HEADS * HEAD_DIM
CONV_K = 4
LEFT_CHUNKS = 8
LEFT = LEFT_CHUNKS * CHUNK
BAND = LEFT + CHUNK
REL_CLIP = 256
NEG = -1e30
LOG2E = 1.4426950408889634

LANES = 128
SUBLANES = 8
VMEM_LIMIT = 54 * 1024 * 1024

GDN_ROWS = 512
GDN_HEADS = 4
GDN_SPLIT = 1
ATT_ROWS = 1024
ATT_HEADS = 4
BAND_PAD = BAND + CHUNK


def _dot(a, b):
    return jnp.dot(a.astype(BF16), b.astype(BF16), preferred_element_type=F32)


def _dot_nt(a, b):
    return lax.dot_general(a.astype(BF16), b.astype(BF16), (((1,), (1,)), ((), ())),
                           preferred_element_type=F32)


def _silu(x):
    return x * jax.nn.sigmoid(x)


def _softplus(x):
    return jnp.maximum(x, 0.0) + jnp.log1p(jnp.exp(-jnp.abs(x)))


def _rms_proj_body(x_ref, nw_ref, *rest, slabs, has_tail, transposed):
    w_refs, rest = rest[:slabs], rest[slabs:]
    mm = _dot_nt if transposed else functools.partial(jnp.dot, preferred_element_type=F32)
    if has_tail:
        wt_ref, o_ref, ot_ref, hn_ref = rest
    else:
        o_ref, hn_ref = rest

    @pl.when(pl.program_id(1) == 0)
    def _():
        x = x_ref[...]
        ms = jnp.mean(x * x, axis=-1, keepdims=True)
        hn = (x * lax.rsqrt(ms + EPS) * nw_ref[...]).astype(BF16)
        hn_ref[...] = hn
        if has_tail:
            ot_ref[...] = _dot_nt(wt_ref[...], hn)

    width = o_ref.shape[1] // slabs
    for p, w_ref in enumerate(w_refs):
        o_ref[:, p * width:(p + 1) * width] = mm(hn_ref[...], w_ref[...]).astype(o_ref.dtype)


def _rms_proj(x, nw, w, n, w_tail=None, *, slabs=1, tm=1024, tn=2048, transposed=False):
    m, d = x.shape
    assert m % tm == 0 and n % tn == 0 and n <= w.shape[0 if transposed else 1] and tn % slabs == 0
    has_tail = w_tail is not None
    col_tiles = n // tn
    flip = (lambda shape: shape[::-1]) if transposed else (lambda shape: shape)
    in_specs = [pl.BlockSpec((tm, d), lambda i, j: (i, 0)),
                pl.BlockSpec((1, d), lambda i, j: (0, 0))]
    in_specs += [pl.BlockSpec(flip((d, tn // slabs)),
                              functools.partial(lambda i, j, p: flip((0, p * col_tiles + j)), p=p))
                 for p in range(slabs)]
    out_specs = [pl.BlockSpec((tm, tn), lambda i, j: (i, j))]
    out_shape = [jax.ShapeDtypeStruct((m, n), BF16)]
    args = [x, nw.reshape(1, d)] + [w] * slabs
    if has_tail:
        assert transposed
        nt = w_tail.shape[0]
        in_specs.append(pl.BlockSpec((nt, d), lambda i, j: (0, 0)))
        out_specs.append(pl.BlockSpec((nt, tm), lambda i, j: (0, i)))
        out_shape.append(jax.ShapeDtypeStruct((nt, m), F32))
        args.append(w_tail)
    res = pl.pallas_call(
        functools.partial(_rms_proj_body, slabs=slabs, has_tail=has_tail, transposed=transposed),
        name="rms_proj_tail" if has_tail else "rms_proj",
        grid=(m // tm, col_tiles),
        in_specs=in_specs, out_specs=out_specs, out_shape=out_shape,
        scratch_shapes=[pltpu.VMEM((tm, d), BF16)],
        compiler_params=pltpu.CompilerParams(
            dimension_semantics=("parallel", "arbitrary"), vmem_limit_bytes=VMEM_LIMIT),
    )(*args)
    return res if has_tail else res[0]


def _out_proj_body(a_ref, w_ref, r_ref, o_ref, wb_ref):
    @pl.when(pl.program_id(0) == 0)
    def _():
        wb_ref[...] = w_ref[...].astype(BF16)

    o_ref[...] = r_ref[...] + jnp.dot(a_ref[...], wb_ref[...], preferred_element_type=F32)


def _out_proj(a, w, res, *, tm=512):
    m, k = a.shape
    n = w.shape[1]
    assert m % tm == 0
    return pl.pallas_call(
        _out_proj_body,
        name="out_proj",
        grid=(m // tm,),
        in_specs=[pl.BlockSpec((tm, k), lambda i: (i, 0)),
                  pl.BlockSpec((k, n), lambda i: (0, 0), pipeline_mode=pl.Buffered(1)),
                  pl.BlockSpec((tm, n), lambda i: (i, 0))],
        out_specs=pl.BlockSpec((tm, n), lambda i: (i, 0)),
        out_shape=jax.ShapeDtypeStruct((m, n), F32),
        scratch_shapes=[pltpu.VMEM((k, n), BF16)],
        compiler_params=pltpu.CompilerParams(
            dimension_semantics=("arbitrary",), vmem_limit_bytes=VMEM_LIMIT),
    )(a, w, res)


def _bdot(a, b):
    return jnp.einsum("cij,cjk->cik", a.astype(BF16), b.astype(BF16), preferred_element_type=F32)


def _bdot_nt(a, b):
    return jnp.einsum("cid,cjd->cij", a.astype(BF16), b.astype(BF16), preferred_element_type=F32)


def _bdot_tn(a, b):
    return jnp.einsum("cjd,cjn->cdn", a.astype(BF16), b.astype(BF16), preferred_element_type=F32)


def _unit_lower_inverse(low, row, col):
    eye = (row == col).astype(F32)
    base = SUBLANES
    ld = jnp.where((row // base) == (col // base), low, 0.0)
    l2 = _bdot(ld, ld)
    l4 = _bdot(l2, l2)
    x = eye - ld
    x = x + _bdot(x, l2)
    x = x + _bdot(x, l4)
    size = base
    low_b = low.astype(BF16)
    while size < CHUNK:
        pair = 2 * size
        off = ((row // pair) == (col // pair)) & ((row % pair) >= size) & ((col % pair) < size)
        c = jnp.where(off, low_b, jnp.zeros_like(low_b))
        x = x - _bdot(x, _bdot(c, x))
        size = pair
    return x


def _gdn_body(alog_ref, dtb_ref, x_ref, cwq_ref, cwk_ref, cwv_ref,
              a_ref, b_ref, onw_ref, o_ref, xq_ref, xk_ref, xv_ref, s_ref):
    rows, width = o_ref.shape
    heads = width // HEAD_DIM
    chunks = rows // CHUNK
    head0 = pl.program_id(1) * heads
    step = pl.program_id(2)

    @pl.when(step == 0)
    def _():
        zeros = jnp.zeros((SUBLANES, width), F32)
        xq_ref[0:SUBLANES, :] = zeros
        xk_ref[0:SUBLANES, :] = zeros
        xv_ref[0:SUBLANES, :] = zeros
        s_ref[...] = jnp.zeros_like(s_ref)

    for part, buf_ref in enumerate((xq_ref, xk_ref, xv_ref)):
        buf_ref[SUBLANES:SUBLANES + rows, :] = x_ref[:, part * width:(part + 1) * width].astype(F32)

    pos = lax.broadcasted_iota(jnp.int32, (SUBLANES, rows), 1) % CHUNK
    sub = lax.broadcasted_iota(jnp.int32, (SUBLANES, HEAD_DIM), 0)
    row = lax.broadcasted_iota(jnp.int32, (CHUNK, CHUNK), 0)
    col = lax.broadcasted_iota(jnp.int32, (CHUNK, CHUNK), 1)

    def gates(hd):
        a_in = jnp.broadcast_to(a_ref[hd], (SUBLANES, rows))
        b_in = jnp.broadcast_to(b_ref[hd], (SUBLANES, rows))
        neg_rate = -jnp.exp(jnp.full((SUBLANES, rows), alog_ref[head0 + hd], F32))
        gc = neg_rate * _softplus(a_in + dtb_ref[head0 + hd])
        beta = jax.nn.sigmoid(b_in)
        shift = 1
        while shift < CHUNK:
            gc = gc + jnp.where(pos >= shift, pltpu.roll(gc, shift, 1), 0.0)
            shift *= 2
        return (jnp.transpose(beta[0:1, :]).reshape(chunks, CHUNK, 1),
                jnp.transpose(gc[0:1, :]).reshape(chunks, CHUNK, 1),
                jnp.stack([gc[0:1, c * CHUNK:(c + 1) * CHUNK] for c in range(chunks)]))

    def conv_silu(buf_ref, cw_ref, hd, c0, c1):
        lanes = slice(hd * HEAD_DIM, (hd + 1) * HEAD_DIM)
        taps = [jnp.broadcast_to(cw_ref[j:j + 1, lanes], (SUBLANES, HEAD_DIM)) for j in range(CONV_K)]
        tiles = []
        for c in range(c0, c1):
            ext = buf_ref[c * CHUNK:(c + 1) * CHUNK + SUBLANES, lanes]
            ext = ext.reshape(CHUNK // SUBLANES + 1, SUBLANES, HEAD_DIM)
            y = ext[1:] * taps[CONV_K - 1]
            for shift in range(1, CONV_K):
                rot = pltpu.roll(ext, shift, 1)
                y = y + jnp.where(sub < shift, rot[:-1], rot[1:]) * taps[CONV_K - 1 - shift]
            tiles.append(_silu(y).reshape(CHUNK, HEAD_DIM))
        return jnp.stack(tiles)

    head_gates = [gates(hd) for hd in range(heads)]

    def chunk_local(c0, c1):
        parts = {name: [] for name in ("q", "k", "v", "beta", "gc", "gcr")}
        for hd in range(heads):
            beta, gc, gcr = head_gates[hd]
            q = conv_silu(xq_ref, cwq_ref, hd, c0, c1)
            k = conv_silu(xk_ref, cwk_ref, hd, c0, c1)
            parts["q"].append(q * lax.rsqrt(jnp.sum(q * q, axis=-1, keepdims=True) + EPS) * (HEAD_DIM ** -0.5))
            parts["k"].append(k * lax.rsqrt(jnp.sum(k * k, axis=-1, keepdims=True) + EPS))
            parts["v"].append(conv_silu(xv_ref, cwv_ref, hd, c0, c1))
            parts["beta"].append(beta[c0:c1])
            parts["gc"].append(gc[c0:c1])
            parts["gcr"].append(gcr[c0:c1])
        q, k, v, beta, gc, gcr = (jnp.concatenate(parts[n], axis=0) for n in ("q", "k", "v", "beta", "gc", "gcr"))
        gcl = gc[:, CHUNK - 1:CHUNK, :]
        decay = jnp.exp(jnp.where(row >= col, gc - gcr, -jnp.inf))
        egc = jnp.exp(gc)
        kb = k * beta
        low = jnp.where(row > col, _bdot_nt(kb, k) * decay, 0.0)
        inv = _unit_lower_inverse(low, row, col)
        sol = _bdot(inv, jnp.concatenate([v * beta, kb * egc], axis=-1))
        qk = _bdot_nt(q, k) * decay
        k_dec = k * jnp.exp(gcl - gc)
        kt = _bdot_tn(k_dec, sol)
        qs = _bdot(qk, sol)
        lhs = jnp.concatenate([kt[:, :, HEAD_DIM:], q * egc - qs[:, :, HEAD_DIM:]], axis=1)
        return lhs, kt[:, :, :HEAD_DIM], qs[:, :, :HEAD_DIM], jnp.exp(gcl)

    per = chunks // GDN_SPLIT
    local = [chunk_local(g * per, (g + 1) * per) for g in range(GDN_SPLIT)]
    for buf_ref in (xq_ref, xk_ref, xv_ref):
        buf_ref[0:SUBLANES, :] = buf_ref[rows:rows + SUBLANES, :]

    onw = onw_ref[...]
    states = [s_ref[hd] for hd in range(heads)]
    for c in range(chunks):
        r0 = c * CHUNK
        for hd in range(heads):
            lhs, add_s, add_o, carry = local[c // per]
            i = hd * per + c % per
            lanes = slice(hd * HEAD_DIM, (hd + 1) * HEAD_DIM)
            prod = _dot(lhs[i], states[hd])
            states[hd] = states[hd] * carry[i] - prod[:HEAD_DIM] + add_s[i]
            o = prod[HEAD_DIM:] + add_o[i]
            zg = _silu(x_ref[r0:r0 + CHUNK, 3 * width + lanes.start:3 * width + lanes.stop].astype(F32))
            o = o * lax.rsqrt(jnp.mean(o * o, axis=-1, keepdims=True) + EPS) * onw
            o_ref[r0:r0 + CHUNK, lanes] = (o * zg).astype(o_ref.dtype)
    for hd in range(heads):
        s_ref[hd] = states[hd]


def _gdn(proj, gates_t, conv_w, a_log, dt_bias, out_norm_w, batch, seq):
    rows, heads = GDN_ROWS, GDN_HEADS
    assert seq % rows == 0 and HEADS % heads == 0
    nt, groups, width = seq // rows, HEADS // heads, heads * HEAD_DIM
    cw = lambda off: pl.BlockSpec((CONV_K, width), lambda b, h, t: (0, off * groups + h))
    gate = lambda off: pl.BlockSpec((heads, 1, rows), lambda b, h, t: (off * groups + h, 0, b * nt + t))
    smem = pl.BlockSpec(memory_space=pltpu.SMEM)
    return pl.pallas_call(
        _gdn_body,
        name="gdn",
        grid=(batch, groups, nt),
        in_specs=[smem, smem,
                  pl.BlockSpec((rows, 4 * width), lambda b, h, t: (b * nt + t, h)),
                  cw(0), cw(1), cw(2),
                  gate(0), gate(1),
                  pl.BlockSpec((1, HEAD_DIM), lambda b, h, t: (0, 0))],
        out_specs=pl.BlockSpec((rows, width), lambda b, h, t: (b * nt + t, h)),
        out_shape=jax.ShapeDtypeStruct((batch * seq, INNER), BF16),
        scratch_shapes=[pltpu.VMEM((rows + SUBLANES, width), F32)] * 3
                       + [pltpu.VMEM((heads, HEAD_DIM, HEAD_DIM), F32)],
        compiler_params=pltpu.CompilerParams(
            dimension_semantics=("parallel", "parallel", "arbitrary"), vmem_limit_bytes=VMEM_LIMIT),
    )(a_log, dt_bias, proj, conv_w, conv_w, conv_w, gates_t, gates_t,
      out_norm_w.reshape(1, HEAD_DIM))


def _attn_body(x_ref, qnw_ref, knw_ref, bias_ref, o_ref, kbuf_ref, vbuf_ref, s_ref):
    rows, width = o_ref.shape
    step = pl.program_id(2)

    def rms(x, w):
        return x * lax.rsqrt(jnp.mean(x * x, axis=-1, keepdims=True) + EPS) * w

    col = lax.broadcasted_iota(jnp.int32, (1, BAND_PAD), 1)

    def run(first_step):
        if first_step:
            kbuf_ref[0:LEFT, :] = jnp.zeros((LEFT, width), BF16)
            vbuf_ref[0:LEFT, :] = jnp.zeros((LEFT, width), BF16)
        vbuf_ref[LEFT:LEFT + rows, :] = x_ref[:, 2 * width:3 * width]
        for hd in range(width // HEAD_DIM):
            lanes = slice(hd * HEAD_DIM, (hd + 1) * HEAD_DIM)
            q_lanes, k_lanes, z_lanes = (slice(part * width + lanes.start, part * width + lanes.stop)
                                         for part in (0, 1, 3))
            q = rms(x_ref[:, q_lanes].astype(F32), qnw_ref[...] * (HEAD_DIM ** -0.5 * LOG2E))
            kbuf_ref[LEFT:LEFT + rows, lanes] = rms(x_ref[:, k_lanes].astype(F32), knw_ref[...]).astype(BF16)
            for lo in range(0, rows, LANES):
                s_ref[hd, lo:lo + LANES, :] = _dot_nt(q[lo:lo + LANES], kbuf_ref[lo:lo + BAND_PAD, lanes])
            for c in range(rows // CHUNK):
                r0 = c * CHUNK
                lo = (c // 2) * LANES
                s = s_ref[hd, r0:r0 + CHUNK, :] + bias_ref[hd, c % 2]
                if first_step and lo < LEFT:
                    s = s + jnp.where(lo + col >= LEFT, 0.0, NEG)
                p = jnp.exp2(s - jnp.max(s, axis=-1, keepdims=True))
                denom = jnp.sum(p, axis=-1, keepdims=True)
                o = _dot(p, vbuf_ref[lo:lo + BAND_PAD, lanes]) / denom
                o = o * _silu(x_ref[r0:r0 + CHUNK, z_lanes].astype(F32))
                o_ref[r0:r0 + CHUNK, lanes] = o.astype(o_ref.dtype)
        kbuf_ref[0:LEFT, :] = kbuf_ref[rows:rows + LEFT, :]
        vbuf_ref[0:LEFT, :] = vbuf_ref[rows:rows + LEFT, :]

    @pl.when(step == 0)
    def _():
        run(True)

    @pl.when(step > 0)
    def _():
        run(False)


def _attn(proj, q_norm_w, k_norm_w, bias_tab, batch, seq):
    rows, heads = ATT_ROWS, ATT_HEADS
    assert seq % rows == 0 and rows >= LEFT and rows % LANES == 0 and HEADS % heads == 0
    nt, groups, width = seq // rows, HEADS // heads, heads * HEAD_DIM
    vec = pl.BlockSpec((1, HEAD_DIM), lambda b, h, t: (0, 0))
    return pl.pallas_call(
        _attn_body,
        name="band_attn",
        grid=(batch, groups, nt),
        in_specs=[pl.BlockSpec((rows, 4 * width), lambda b, h, t: (b * nt + t, h)), vec, vec,
                  pl.BlockSpec((heads, 2, CHUNK, BAND_PAD), lambda b, h, t: (h, 0, 0, 0))],
        out_specs=pl.BlockSpec((rows, width), lambda b, h, t: (b * nt + t, h)),
        out_shape=jax.ShapeDtypeStruct((batch * seq, INNER), BF16),
        scratch_shapes=[pltpu.VMEM((LEFT + rows, width), BF16),
                        pltpu.VMEM((LEFT + rows, width), BF16),
                        pltpu.VMEM((heads, rows, BAND_PAD), F32)],
        compiler_params=pltpu.CompilerParams(
            dimension_semantics=("parallel", "parallel", "arbitrary"), vmem_limit_bytes=VMEM_LIMIT),
    )(proj, q_norm_w.reshape(1, HEAD_DIM), k_norm_w.reshape(1, HEAD_DIM), bias_tab)


def _bias_table(rel_bias):
    t = jnp.arange(BAND + CHUNK - 1)
    idx = jnp.clip(LEFT + CHUNK - 1 - t, -REL_CLIP, REL_CLIP) + REL_CLIP
    diag = rel_bias.astype(F32)[:, idx] * LOG2E
    band = jnp.stack([diag[:, CHUNK - 1 - r:CHUNK - 1 - r + BAND] for r in range(CHUNK)], axis=1)
    pad = jnp.full((HEADS, CHUNK, CHUNK), NEG, F32)
    even = jnp.concatenate([band, pad], axis=-1)
    odd = jnp.concatenate([pad, band], axis=-1)
    return jnp.stack([even, odd], axis=1)


def kernel(x, norm_w, a_w_in, a_conv_w, a_a_log, a_dt_bias, a_out_norm_w, a_w_out,
           b_w_in, b_q_norm_w, b_k_norm_w, b_rel_bias, b_w_out):
    batch, seq, d = x.shape
    h0 = x.reshape(batch * seq, d)

    w_in = jnp.swapaxes(a_w_in[0], 0, 1)
    w_tail = jnp.pad(w_in[4 * INNER:], ((0, LANES - 2 * HEADS), (0, 0))).astype(BF16)
    proj, tail = _rms_proj(h0, norm_w[0], w_in.astype(BF16), 4 * INNER, w_tail,
                           slabs=4, tn=4 * GDN_HEADS * HEAD_DIM, transposed=True)
    mixed = _gdn(proj, tail.reshape(LANES, 1, batch * seq), a_conv_w[0], a_a_log[0], a_dt_bias[0], a_out_norm_w[0], batch, seq)
    h1 = _out_proj(mixed, a_w_out[0], h0)

    proj = _rms_proj(h1, norm_w[1], b_w_in[0].astype(BF16), 4 * INNER, slabs=4, tn=4 * ATT_HEADS * HEAD_DIM)
    mixed = _attn(proj, b_q_norm_w[0], b_k_norm_w[0], _bias_table(b_rel_bias[0]), batch, seq)
    h2 = _out_proj(mixed, b_w_out[0], h1)
    return h2.reshape(batch, seq, d)
```
